```python
import jax, jax.numpy as jnp
from jax import lax
import numpy as np

D_MODEL = 1024
BATCH = 2
SEQ = 16384
DEPTH = 4

CHUNK = 64
N_LEFT_CHUNKS = 8
BAND = (N_LEFT_CHUNKS + 1) * CHUNK
REL_MAX = 4 * CHUNK
N_REL = REL_MAX + CHUNK
ATT_HEADS = 8
HEAD_DIM = 64
ATT_WIDTH = ATT_HEADS * HEAD_DIM
CONV_WIDTH = D_MODEL // 2
CONV_TAPS = 31
EVEN_IN = 2 * CONV_WIDTH + 3 * ATT_WIDTH
EVEN_MIX = CONV_WIDTH + ATT_WIDTH
GMLP_CHUNK = 128
GMLP_GROUPS = 8
GMLP_WIDTH = 2 * D_MODEL
GMLP_GROUP_DIM = GMLP_WIDTH // GMLP_GROUPS
D_FF = 2816
FFN_TAPS = 3
EPS = 1e-6
N_EVEN = (DEPTH + 1) // 2
N_ODD = DEPTH // 2

kernel_name = "hybrid_conformer_chunkattn_gmlp_convffn"


def rms_norm(x, g):
    xf = x.astype(jnp.float32)
    y = xf * lax.rsqrt(jnp.mean(xf * xf, axis=-1, keepdims=True) + EPS)
    return (y * g.astype(jnp.float32)).astype(x.dtype)


def layer_norm(x, g, b):
    xf = x.astype(jnp.float32)
    mu = jnp.mean(xf, axis=-1, keepdims=True)
    var = jnp.mean(jnp.square(xf - mu), axis=-1, keepdims=True)
    y = (xf - mu) * lax.rsqrt(var + EPS) * g.astype(jnp.float32) + b.astype(jnp.float32)
    return y.astype(x.dtype)


def causal_dwconv(x, w, b):
    taps = w.shape[0]
    y = lax.conv_general_dilated(
        x, w[:, None, :].astype(x.dtype), window_strides=(1,),
        padding=[(taps - 1, 0)], dimension_numbers=('NWC', 'WIO', 'NWC'),
        feature_group_count=x.shape[-1])
    return y + b.astype(x.dtype)


def band_rel_index():
    qpos = np.arange(CHUNK)[:, None] + N_LEFT_CHUNKS * CHUNK
    kpos = np.arange(BAND)[None, :]
    return np.clip(qpos - kpos, -(CHUNK - 1), REL_MAX) + (CHUNK - 1)


def chunked_rel_attention(q, k, v, qn_g, kn_g, rel_bias):
    B, S, _ = q.shape
    nc = S // CHUNK
    q = rms_norm(q.reshape(B, nc, CHUNK, ATT_HEADS, HEAD_DIM), qn_g)
    k = rms_norm(k.reshape(B, S, ATT_HEADS, HEAD_DIM), kn_g)
    v = v.reshape(B, S, ATT_HEADS, HEAD_DIM)
    pad = [(0, 0), (N_LEFT_CHUNKS * CHUNK, 0), (0, 0), (0, 0)]
    kp = jnp.pad(k, pad).reshape(B, nc + N_LEFT_CHUNKS, CHUNK, ATT_HEADS, HEAD_DIM)
    vp = jnp.pad(v, pad).reshape(B, nc + N_LEFT_CHUNKS, CHUNK, ATT_HEADS, HEAD_DIM)
    band = jnp.arange(nc)[:, None] + jnp.arange(N_LEFT_CHUNKS + 1)[None, :]
    kb = kp[:, band].reshape(B, nc, BAND, ATT_HEADS, HEAD_DIM)
    vb = vp[:, band].reshape(B, nc, BAND, ATT_HEADS, HEAD_DIM)
    s = jnp.einsum('bcqhd,bckhd->bchqk', q, kb).astype(jnp.float32) * (HEAD_DIM ** -0.5)
    bias = rel_bias[:, band_rel_index()].astype(jnp.float32)
    key_chunk = jnp.arange(nc)[:, None] + (jnp.arange(BAND) // CHUNK)[None, :] - N_LEFT_CHUNKS
    valid = (key_chunk >= 0)[None, :, None, None, :]
    s = jnp.where(valid, s + bias[None, None], -1e30)
    p = jax.nn.softmax(s, axis=-1).astype(v.dtype)
    o = jnp.einsum('bchqk,bckhd->bcqhd', p, vb)
    return o.reshape(B, S, ATT_WIDTH)


def conv_attn_mixer(h, w_in, conv_w, conv_b, conv_ln_g, conv_ln_b, qn_g, kn_g, rel_bias, w_out):
    z = h @ w_in
    a_lin, a_gate, q, k, v = jnp.split(
        z, [CONV_WIDTH, 2 * CONV_WIDTH, 2 * CONV_WIDTH + ATT_WIDTH,
            2 * CONV_WIDTH + 2 * ATT_WIDTH], axis=-1)
    a = a_lin * jax.nn.sigmoid(a_gate)
    a = causal_dwconv(a, conv_w, conv_b)
    a = jax.nn.silu(layer_norm(a, conv_ln_g, conv_ln_b))
    o = chunked_rel_attention(q, k, v, qn_g, kn_g, rel_bias)
    return jnp.concatenate([a, o], axis=-1) @ w_out


def gmlp_mixer(h, w_in, gate_ln_g, gate_ln_b, w_sp, b_sp, w_out):
    B, S, _ = h.shape
    z = jax.nn.gelu(h @ w_in)
    u, v = jnp.split(z, 2, axis=-1)
    v = layer_norm(v, gate_ln_g, gate_ln_b)
    nc = S // GMLP_CHUNK
    v = v.reshape(B, nc, GMLP_CHUNK, GMLP_GROUPS, GMLP_GROUP_DIM)
    causal = jnp.tril(jnp.ones((GMLP_CHUNK, GMLP_CHUNK), dtype=bool))
    w = jnp.where(causal[None], w_sp, 0.0).astype(v.dtype)
    sv = jnp.einsum('gts,bcsgd->bctgd', w, v) + b_sp.T.astype(v.dtype)[None, None, :, :, None]
    y = u * sv.reshape(B, S, GMLP_WIDTH)
    return y @ w_out


def conv_ffn(h, w_gate, w_up, dw_w, dw_b, w_down):
    g = causal_dwconv(h @ w_gate, dw_w, dw_b)
    return (jax.nn.silu(g) * (h @ w_up)) @ w_down


def setup_inputs(seed: int = 0) -> dict:
    key = jax.random.key(seed)
    ks = jax.random.split(key, 24)

    def nrm(k, shape, scale):
        return jax.random.normal(k, shape, jnp.float32) * scale

    return {
        "x": nrm(ks[0], (BATCH, SEQ, D_MODEL), 1.0),
        "mix_norm_g": 1.0 + nrm(ks[1], (DEPTH, D_MODEL), 0.05),
        "w_in_e": nrm(ks[2], (N_EVEN, D_MODEL, EVEN_IN), D_MODEL ** -0.5),
        "conv_w": nrm(ks[3], (N_EVEN, CONV_TAPS, CONV_WIDTH), CONV_TAPS ** -0.5),
        "conv_b": nrm(ks[4], (N_EVEN, CONV_WIDTH), 0.02),
        "conv_ln_g": 1.0 + nrm(ks[5], (N_EVEN, CONV_WIDTH), 0.05),
        "conv_ln_b": nrm(ks[6], (N_EVEN, CONV_WIDTH), 0.02),
        "qn_g": 1.0 + nrm(ks[7], (N_EVEN, HEAD_DIM), 0.05),
        "kn_g": 1.0 + nrm(ks[8], (N_EVEN, HEAD_DIM), 0.05),
        "rel_bias": nrm(ks[9], (N_EVEN, ATT_HEADS, N_REL), 0.5),
        "w_out_e": nrm(ks[10], (N_EVEN, EVEN_MIX, D_MODEL), EVEN_MIX ** -0.5),
        "w_in_o": nrm(ks[11], (N_ODD, D_MODEL, 2 * GMLP_WIDTH), D_MODEL ** -0.5),
        "gate_ln_g": 1.0 + nrm(ks[12], (N_ODD, GMLP_WIDTH), 0.05),
        "gate_ln_b": nrm(ks[13], (N_ODD, GMLP_WIDTH), 0.02),
        "w_sp": nrm(ks[14], (N_ODD, GMLP_GROUPS, GMLP_CHUNK, GMLP_CHUNK), GMLP_CHUNK ** -0.5),
        "b_sp": 1.0 + nrm(ks[15], (N_ODD, GMLP_GROUPS, GMLP_CHUNK), 0.1),
        "w_out_o": nrm(ks[16], (N_ODD, GMLP_WIDTH, D_MODEL), GMLP_WIDTH ** -0.5),
        "ffn_norm_g": 1.0 + nrm(ks[17], (DEPTH, D_MODEL), 0.05),
        "w_gate": nrm(ks[18], (DEPTH, D_MODEL, D_FF), D_MODEL ** -0.5),
        "w_up": nrm(ks[19], (DEPTH, D_MODEL, D_FF), D_MODEL ** -0.5),
        "ffn_dw_w": nrm(ks[20], (DEPTH, FFN_TAPS, D_FF), FFN_TAPS ** -0.5),
        "ffn_dw_b": nrm(ks[21], (DEPTH, D_FF), 0.02),
        "w_down": nrm(ks[22], (DEPTH, D_FF, D_MODEL), D_FF ** -0.5),
    }


def reference(x, mix_norm_g, w_in_e, conv_w, conv_b, conv_ln_g, conv_ln_b, qn_g, kn_g,
              rel_bias, w_out_e, w_in_o, gate_ln_g, gate_ln_b, w_sp, b_sp, w_out_o,
              ffn_norm_g, w_gate, w_up, ffn_dw_w, ffn_dw_b, w_down):
    for layer in range(DEPTH):
        h = rms_norm(x, mix_norm_g[layer])
        if layer % 2 == 0:
            e = layer // 2
            x = x + conv_attn_mixer(h, w_in_e[e], conv_w[e], conv_b[e], conv_ln_g[e],
                                    conv_ln_b[e], qn_g[e], kn_g[e], rel_bias[e], w_out_e[e])
        else:
            o = layer // 2
            x = x + gmlp_mixer(h, w_in_o[o], gate_ln_g[o], gate_ln_b[o], w_sp[o], b_sp[o],
                               w_out_o[o])
        h = rms_norm(x, ffn_norm_g[layer])
        x = x + conv_ffn(h, w_gate[layer], w_up[layer], ffn_dw_w[layer], ffn_dw_b[layer],
                         w_down[layer])
    return x
```

```python
import functools

import numpy as np
import jax
import jax.numpy as jnp
from jax import lax
from jax.experimental import pallas as pl
from jax.experimental.pallas import tpu as pltpu

F32 = jnp.float32
BF16 = jnp.bfloat16

EPS = 1e-6
CHUNK = 64
N_LEFT_CHUNKS = 8
REL_MAX = 4 * CHUNK
ATT_HEADS = 8
HEAD_DIM = 64
ATT_WIDTH = ATT_HEADS * HEAD_DIM
CONV_TAPS = 31
GMLP_CHUNK = 128
GMLP_GROUPS = 8
FFN_TAPS = 3
NEG_INF = -1e30

SUBLANES = 8
LANES = 128
VMEM_LIMIT_CAP_BYTES = 60000 * 1024

ROW_TILE = 512
ATT_TILE = 256
ATT_KEYS = 3 * ATT_TILE
FF_CHUNK = 256
CONV_HALO = 32
CONV_SUB = 64


def _dot(a, b):
    return jnp.dot(a, b, preferred_element_type=F32)


def _dot_nt(a, b):
    return lax.dot_general(a, b, (((1,), (1,)), ((), ())), preferred_element_type=F32)


def _rms(x, g):
    return x * lax.rsqrt(jnp.mean(x * x, axis=-1, keepdims=True) + EPS) * g


def _nbytes(shape, dtype):
    return int(np.prod(shape)) * jnp.dtype(dtype).itemsize


def _vmem_limit(pipelined, resident, scratch):
    need = 2 * sum(pipelined) + sum(resident) + sum(scratch)
    need += 4 * max(pipelined)
    return int(min(VMEM_LIMIT_CAP_BYTES, max(need, 16 * 1024 * 1024)))


def _resident(shape):
    nd = len(shape)
    return pl.BlockSpec(shape, lambda *_: (0,) * nd, pipeline_mode=pl.Buffered(1))


def _even_in_kernel(x_ref, g_ref, w_ref, hm_ref, qg_ref, kg_ref,
                    a_ref, q_ref, k_ref, v_ref, h_ref):
    cw = a_ref.shape[-1]
    x = x_ref[...]
    h_ref[...] = _rms(x, g_ref[...]).astype(BF16)

    a_lin = _dot(h_ref[...], w_ref[:, 0:cw])
    a_gate = _dot(h_ref[...], w_ref[:, cw:2 * cw])
    a_ref[...] = a_lin * jax.nn.sigmoid(a_gate)

    def head_norm(z, gain):
        s = z * z
        hi = s.astype(BF16)
        lo = (s - hi.astype(F32)).astype(BF16)
        ss = _dot(hi, hm_ref[...]) + _dot(lo, hm_ref[...])
        return z * lax.rsqrt(ss * (1.0 / HEAD_DIM) + EPS) * gain

    base = 2 * cw
    q = _dot(h_ref[...], w_ref[:, base:base + ATT_WIDTH])
    q_ref[...] = (head_norm(q, qg_ref[...]) * (HEAD_DIM ** -0.5)).astype(BF16)
    k = _dot(h_ref[...], w_ref[:, base + ATT_WIDTH:base + 2 * ATT_WIDTH])
    k_ref[...] = head_norm(k, kg_ref[...]).astype(BF16)
    v = _dot(h_ref[...], w_ref[:, base + 2 * ATT_WIDTH:base + 3 * ATT_WIDTH])
    v_ref[...] = v.astype(BF16)


def _even_in(x2, g, w_bf, head_ones, qg, kg):
    t, d = x2.shape
    n = w_bf.shape[1]
    cw = (n - 3 * ATT_WIDTH) // 2
    tm = ROW_TILE
    row = lambda width: pl.BlockSpec((tm, width), lambda i: (i, 0))
    limit = _vmem_limit(
        [_nbytes((tm, d), F32), _nbytes((tm, cw), F32)] + 3 * [_nbytes((tm, ATT_WIDTH), BF16)],
        [_nbytes(w_bf.shape, BF16), _nbytes(head_ones.shape, BF16)],
        [_nbytes((tm, d), BF16), 4 * _nbytes((tm, 2 * cw), F32)])
    return pl.pallas_call(
        _even_in_kernel,
        grid=(t // tm,),
        in_specs=[row(d), _resident((1, d)), _resident(w_bf.shape), _resident(head_ones.shape),
                  _resident((1, ATT_WIDTH)), _resident((1, ATT_WIDTH))],
        out_specs=[row(cw), row(ATT_WIDTH), row(ATT_WIDTH), row(ATT_WIDTH)],
        out_shape=[jax.ShapeDtypeStruct((t, cw), F32)] +
                  3 * [jax.ShapeDtypeStruct((t, ATT_WIDTH), BF16)],
        scratch_shapes=[pltpu.VMEM((tm, d), BF16)],
        compiler_params=pltpu.CompilerParams(
            dimension_semantics=("parallel",), vmem_limit_bytes=limit),
        name="even_in",
    )(x2, g, w_bf, head_ones, qg, kg)


def _conv_module_kernel(prev_ref, cur_ref, w_ref, b_ref, lg_ref, lb_ref, o_ref, buf_ref):
    ts = cur_ref.shape[0]
    halo = prev_ref.shape[0]
    taps = w_ref.shape[0]
    first = pl.program_id(1) == 0
    buf_ref[0:halo, :] = jnp.where(first, 0.0, prev_ref[...])
    buf_ref[halo:halo + ts, :] = cur_ref[...]
    off = halo - (taps - 1)
    for r in range(0, ts, CONV_SUB):
        acc = jnp.zeros((CONV_SUB, cur_ref.shape[1]), F32) + b_ref[...]
        for k in range(taps):
            acc = acc + w_ref[k:k + 1, :] * buf_ref[r + off + k:r + off + k + CONV_SUB, :]
        mu = jnp.mean(acc, axis=-1, keepdims=True)
        cen = acc - mu
        var = jnp.mean(cen * cen, axis=-1, keepdims=True)
        y = cen * lax.rsqrt(var + EPS) * lg_ref[...] + lb_ref[...]
        o_ref[r:r + CONV_SUB, :] = (y * jax.nn.sigmoid(y)).astype(o_ref.dtype)


def _conv_module(a2, batch, conv_w, conv_b, ln_g, ln_b):
    t, c = a2.shape
    ts = ROW_TILE
    nt = t // batch // ts
    per = ts // CONV_HALO
    limit = _vmem_limit([_nbytes((ts, c), F32), _nbytes((CONV_HALO, c), F32),
                         _nbytes((ts, c), BF16)],
                        [_nbytes((32, c), F32)], [_nbytes((ts + CONV_HALO, c), F32)])
    return pl.pallas_call(
        _conv_module_kernel,
        grid=(batch, nt),
        in_specs=[
            pl.BlockSpec((CONV_HALO, c), lambda b, i: (jnp.maximum((b * nt + i) * per - 1, 0), 0)),
            pl.BlockSpec((ts, c), lambda b, i: (b * nt + i, 0)),
            _resident(conv_w.shape), _resident((1, c)), _resident((1, c)), _resident((1, c))],
        out_specs=pl.BlockSpec((ts, c), lambda b, i: (b * nt + i, 0)),
        out_shape=jax.ShapeDtypeStruct((t, c), BF16),
        scratch_shapes=[pltpu.VMEM((ts + CONV_HALO, c), F32)],
        compiler_params=pltpu.CompilerParams(
            dimension_semantics=("parallel", "parallel"), vmem_limit_bytes=limit),
        name="conv_module",
    )(a2, a2, conv_w, conv_b, ln_g, ln_b)


def _attn_kernel(q_ref, k0_ref, k1_ref, k2_ref, v0_ref, v1_ref, v2_ref, bias_ref, o_ref):
    low = lax.broadcasted_iota(jnp.int32, (1, 2 * HEAD_DIM), 1) < HEAD_DIM
    keep = [jnp.where(low, 1.0, 0.0).astype(BF16), jnp.where(low, 0.0, 1.0).astype(BF16)]
    for p in range(ATT_HEADS // 2):
        cols = slice(2 * HEAD_DIM * p, 2 * HEAD_DIM * (p + 1))
        qp = q_ref[:, cols]
        kp = jnp.concatenate([k0_ref[:, cols], k1_ref[:, cols], k2_ref[:, cols]], axis=0)
        vp = jnp.concatenate([v0_ref[:, cols], v1_ref[:, cols], v2_ref[:, cols]], axis=0)
        halves = []
        for e in range(2):
            s = _dot_nt(qp * keep[e], kp) + bias_ref[0, 2 * p + e]
            m = jnp.max(s, axis=-1, keepdims=True)
            pe = jnp.exp(s - m)
            den = jnp.sum(pe, axis=-1, keepdims=True)
            halves.append(_dot(pe.astype(BF16), vp) * (1.0 / den))
        o_ref[:, cols] = jnp.where(low, halves[0], halves[1]).astype(o_ref.dtype)


def _attn_bias_table(rel_bias):
    t = np.arange(ATT_TILE)[:, None]
    kk = np.arange(ATT_KEYS)[None, :]
    idx = np.clip(t + 2 * ATT_TILE - kk, -(CHUNK - 1), REL_MAX) + (CHUNK - 1)
    rel_chunk = kk // CHUNK - t // CHUNK
    band = (rel_chunk >= 0) & (rel_chunk <= N_LEFT_CHUNKS)
    variants = np.stack([band & (kk >= (2 - v) * ATT_TILE) for v in range(3)])
    gathered = rel_bias[:, idx].astype(F32)
    return jnp.where(variants[:, None], gathered[None], NEG_INF)


def _attention(q2, k2, v2, batch, bias_tab):
    t, w = q2.shape
    tq = ATT_TILE
    nt = t // batch // tq
    blk = lambda back: pl.BlockSpec(
        (tq, w), lambda b, i: (b * nt + jnp.maximum(i - back, 0), 0))
    limit = _vmem_limit(8 * [_nbytes((tq, w), BF16)] + [_nbytes(bias_tab.shape[1:], F32)],
                        [], [6 * _nbytes((tq, ATT_KEYS), F32)])
    return pl.pallas_call(
        _attn_kernel,
        grid=(batch, nt),
        in_specs=[blk(0), blk(2), blk(1), blk(0), blk(2), blk(1), blk(0),
                  pl.BlockSpec((1,) + bias_tab.shape[1:],
                               lambda b, i: (jnp.minimum(i, 2), 0, 0, 0))],
        out_specs=blk(0),
        out_shape=jax.ShapeDtypeStruct((t, w), BF16),
        compiler_params=pltpu.CompilerParams(
            dimension_semantics=("parallel", "parallel"), vmem_limit_bytes=limit),
        name="chunk_attention",
    )(q2, k2, k2, k2, v2, v2, v2, bias_tab)


def _gmlp_in_kernel(x_ref, g_ref, w_ref, lg_ref, lb_ref, wsp_ref, bsp_ref, y_ref,
                    h_ref, v_ref, vn_ref):
    tm = x_ref.shape[0]
    width = v_ref.shape[1]
    gd = width // GMLP_GROUPS
    h_ref[...] = _rms(x_ref[...], g_ref[...]).astype(BF16)

    total = jnp.zeros((tm, 1), F32)
    for g in range(GMLP_GROUPS):
        zv = jax.nn.gelu(_dot(h_ref[...], w_ref[:, width + g * gd:width + (g + 1) * gd]))
        v_ref[:, g * gd:(g + 1) * gd] = zv
        total = total + jnp.sum(zv, axis=-1, keepdims=True)
    mu = total * (1.0 / width)
    sq = jnp.zeros((tm, 1), F32)
    for g in range(GMLP_GROUPS):
        cen = v_ref[:, g * gd:(g + 1) * gd] - mu
        sq = sq + jnp.sum(cen * cen, axis=-1, keepdims=True)
    rstd = lax.rsqrt(sq * (1.0 / width) + EPS)
    for g in range(GMLP_GROUPS):
        sl = slice(g * gd, (g + 1) * gd)
        vn_ref[:, sl] = ((v_ref[:, sl] - mu) * rstd * lg_ref[:, sl] + lb_ref[:, sl]).astype(BF16)

    row = lax.broadcasted_iota(jnp.int32, (GMLP_CHUNK, GMLP_CHUNK), 0)
    col = lax.broadcasted_iota(jnp.int32, (GMLP_CHUNK, GMLP_CHUNK), 1)
    causal = row >= col
    for g in range(GMLP_GROUPS):
        sl = slice(g * gd, (g + 1) * gd)
        u = jax.nn.gelu(_dot(h_ref[...], w_ref[:, sl]))
        wm = jnp.where(causal, wsp_ref[g], 0.0).astype(BF16)
        bias = jnp.concatenate([bsp_ref[g]] * (gd // LANES), axis=-1)
        for r in range(0, tm, GMLP_CHUNK):
            sv = _dot(wm, vn_ref[r:r + GMLP_CHUNK, sl]) + bias
            y_ref[r:r + GMLP_CHUNK, sl] = (u[r:r + GMLP_CHUNK] * sv).astype(y_ref.dtype)


def _gmlp_in(x2, g, w_bf, ln_g, ln_b, w_sp, b_sp_tile):
    t, d = x2.shape
    width = w_bf.shape[1] // 2
    tm = ROW_TILE
    limit = _vmem_limit(
        [_nbytes((tm, d), F32), _nbytes((tm, width), BF16)],
        [_nbytes(w_bf.shape, BF16), _nbytes(w_sp.shape, F32), _nbytes(b_sp_tile.shape, F32)],
        [_nbytes((tm, d), BF16), _nbytes((tm, width), F32), _nbytes((tm, width), BF16),
         8 * _nbytes((tm, width // GMLP_GROUPS), F32)])
    return pl.pallas_call(
        _gmlp_in_kernel,
        grid=(t // tm,),
        in_specs=[pl.BlockSpec((tm, d), lambda i: (i, 0)), _resident((1, d)),
                  _resident(w_bf.shape), _resident((1, width)), _resident((1, width)),
                  _resident(w_sp.shape), _resident(b_sp_tile.shape)],
        out_specs=pl.BlockSpec((tm, width), lambda i: (i, 0)),
        out_shape=jax.ShapeDtypeStruct((t, width), BF16),
        scratch_shapes=[pltpu.VMEM((tm, d), BF16), pltpu.VMEM((tm, width), F32),
                        pltpu.VMEM((tm, width), BF16)],
        compiler_params=pltpu.CompilerParams(
            dimension_semantics=("parallel",), vmem_limit_bytes=limit),
        name="gmlp_in",
    )(x2, g, w_bf, ln_g, ln_b, w_sp, b_sp_tile)


def _mix_ffn_kernel(n_mix, *refs):
    x_ref = refs[0]
    y_refs = refs[1:1 + n_mix]
    wm_refs = refs[1 + n_mix:1 + 2 * n_mix]
    (g_ref, wg_ref, wu_ref, dww_ref, dwb_ref, wd_ref,
     o_ref, h_ref, act_ref, gbuf_ref, carry_ref) = refs[1 + 2 * n_mix:]
    tm = x_ref.shape[0]
    d_ff = wg_ref.shape[1]
    fc = gbuf_ref.shape[1]

    @pl.when(pl.program_id(1) == 0)
    def _():
        carry_ref[...] = jnp.zeros_like(carry_ref)

    x1 = x_ref[...]
    for y_ref, wm_ref in zip(y_refs, wm_refs):
        x1 = x1 + _dot(y_ref[...], wm_ref[...])
    o_ref[...] = x1
    h_ref[...] = _rms(x1, g_ref[...]).astype(BF16)

    for c in range(0, d_ff, fc):
        cols = slice(c, c + fc)
        gate = _dot(h_ref[...], wg_ref[:, cols])
        up = _dot(h_ref[...], wu_ref[:, cols])
        gbuf_ref[0:SUBLANES, :] = carry_ref[:, cols]
        gbuf_ref[SUBLANES:SUBLANES + tm, :] = gate
        carry_ref[:, cols] = gate[tm - SUBLANES:tm, :]
        conv = dwb_ref[:, cols]
        for k in range(FFN_TAPS):
            lo = SUBLANES - (FFN_TAPS - 1) + k
            conv = conv + dww_ref[k:k + 1, cols] * gbuf_ref[lo:lo + tm, :]
        act_ref[:, cols] = (conv * jax.nn.sigmoid(conv) * up).astype(BF16)

    o_ref[...] = o_ref[...] + _dot(act_ref[...], wd_ref[...])


def _mix_ffn(x2, batch, mix_pairs, g, wg, wu, dww, dwb, wd):
    t, d = x2.shape
    tm = ROW_TILE
    nt = t // batch // tm
    d_ff = wg.shape[1]
    fc = FF_CHUNK
    n_mix = len(mix_pairs)
    row = lambda width: pl.BlockSpec((tm, width), lambda b, i: (b * nt + i, 0))
    ys = [y for y, _ in mix_pairs]
    wms = [w for _, w in mix_pairs]
    limit = _vmem_limit(
        2 * [_nbytes((tm, d), F32)] + [_nbytes((tm, y.shape[1]), BF16) for y in ys],
        [_nbytes(w.shape, BF16) for w in wms] + 3 * [_nbytes(wg.shape, BF16)] +
        [_nbytes((16, d_ff), F32)],
        [_nbytes((tm, d), BF16), _nbytes((tm, d_ff), BF16),
         _nbytes((tm + SUBLANES, fc), F32), _nbytes((SUBLANES, d_ff), F32),
         4 * _nbytes((tm, fc), F32)])
    return pl.pallas_call(
        functools.partial(_mix_ffn_kernel, n_mix),
        grid=(batch, nt),
        in_specs=[row(d)] + [row(y.shape[1]) for y in ys] + [_resident(w.shape) for w in wms] +
                 [_resident((1, d)), _resident(wg.shape), _resident(wu.shape),
                  _resident(dww.shape), _resident(dwb.shape), _resident(wd.shape)],
        out_specs=row(d),
        out_shape=jax.ShapeDtypeStruct((t, d), F32),
        scratch_shapes=[pltpu.VMEM((tm, d), BF16), pltpu.VMEM((tm, d_ff), BF16),
                        pltpu.VMEM((tm + SUBLANES, fc), F32),
                        pltpu.VMEM((SUBLANES, d_ff), F32)],
        compiler_params=pltpu.CompilerParams(
            dimension_semantics=("arbitrary", "arbitrary"), vmem_limit_bytes=limit),
        name="mix_ffn",
    )(x2, *ys, *wms, g, wg, wu, dww, dwb, wd)


def kernel(x, mix_norm_g, w_in_e, conv_w, conv_b, conv_ln_g, conv_ln_b, qn_g, kn_g, rel_bias,
           w_out_e, w_in_o, gate_ln_g, gate_ln_b, w_sp, b_sp, w_out_o, ffn_norm_g, w_gate, w_up,
           ffn_dw_w, ffn_dw_b, w_down):
    batch, seq, d = x.shape
    depth = mix_norm_g.shape[0]
    d_ff = w_gate.shape[-1]
    assert seq % ROW_TILE == 0 and seq % ATT_TILE == 0 and d_ff % FF_CHUNK == 0
    assert ROW_TILE % GMLP_CHUNK == 0 and ROW_TILE % CONV_HALO == 0 and ATT_TILE % CHUNK == 0
    assert ATT_KEYS >= ATT_TILE + N_LEFT_CHUNKS * CHUNK and CONV_HALO >= CONV_TAPS - 1

    x2 = x.reshape(batch * seq, d)
    head_ones = jnp.asarray(np.kron(np.eye(ATT_HEADS), np.ones((HEAD_DIM, HEAD_DIM))), BF16)
    row = lambda vec: vec.reshape(1, -1).astype(F32)

    for layer in range(depth):
        i = layer // 2
        if layer % 2 == 0:
            a, q, k, v = _even_in(
                x2, row(mix_norm_g[layer]), w_in_e[i].astype(BF16), head_ones,
                row(jnp.tile(qn_g[i], ATT_HEADS)), row(jnp.tile(kn_g[i], ATT_HEADS)))
            a = _conv_module(a, batch, conv_w[i].astype(F32), row(conv_b[i]),
                             row(conv_ln_g[i]), row(conv_ln_b[i]))
            o = _attention(q, k, v, batch, _attn_bias_table(rel_bias[i]))
            cw = a.shape[1]
            w_out = w_out_e[i].astype(BF16)
            mix_pairs = [(a, w_out[:cw]), (o, w_out[cw:])]
        else:
            b_tile = jnp.broadcast_to(b_sp[i].astype(F32)[:, :, None],
                                      (GMLP_GROUPS, GMLP_CHUNK, LANES))
            y = _gmlp_in(x2, row(mix_norm_g[layer]), w_in_o[i].astype(BF16),
                         row(gate_ln_g[i]), row(gate_ln_b[i]), w_sp[i].astype(F32), b_tile)
            mix_pairs = [(y, w_out_o[i].astype(BF16))]
        x2 = _mix_ffn(
            x2, batch, mix_pairs, row(ffn_norm_g[layer]),
            w_gate[layer].astype(BF16), w_up[layer].astype(BF16),
            ffn_dw_w[layer].astype(F32), row(ffn_dw_b[layer]), w_down[layer].astype(BF16))
    return x2.reshape(batch, seq, d)
```

```python
import functools

import numpy as np
import jax
import jax.numpy as jnp
from jax import lax
from jax.experimental import pallas as pl
from jax.experimental.pallas import tpu as pltpu

F32 = jnp.float32
BF16 = jnp.bfloat16

EPS = 1e-6
CHUNK = 64
N_LEFT_CHUNKS = 8
REL_MAX = 4 * CHUNK
ATT_HEADS = 8
HEAD_DIM = 64
ATT_WIDTH = ATT_HEADS * HEAD_DIM
CONV_TAPS = 31
GMLP_CHUNK = 128
GMLP_GROUPS = 8
FFN_TAPS = 3
NEG_INF = -1e30
LOG2E = 1.4426950408889634

SUBLANES = 8
LANES = 128
VMEM_LIMIT_CAP_BYTES = 60000 * 1024

ROW_TILE = 512
ATT_TILE = 256
ATT_KEYS = 3 * ATT_TILE
FF_CHUNK = 256
CONV_HALO = 32
CONV_SUB = 64


def _dot(a, b):
    return jnp.dot(a, b, preferred_element_type=F32)


def _dot_nt(a, b):
    return lax.dot_general(a, b, (((1,), (1,)), ((), ())), preferred_element_type=F32)


def _rms(x, g):
    return x * lax.rsqrt(jnp.mean(x * x, axis=-1, keepdims=True) + EPS) * g


def _gelu_tanh(x):
    a = float(-2.0 * np.sqrt(2.0 / np.pi) * LOG2E)
    b = a * 0.044715
    return x * (1.0 / (1.0 + jnp.exp2(x * (b * (x * x) + a))))


def _nbytes(shape, dtype):
    return int(np.prod(shape)) * jnp.dtype(dtype).itemsize


def _vmem_limit(pipelined, resident, scratch):
    need = 2 * sum(pipelined) + sum(resident) + sum(scratch)
    need += 4 * max(pipelined)
    return int(min(VMEM_LIMIT_CAP_BYTES, max(need, 16 * 1024 * 1024)))


def _resident(shape):
    nd = len(shape)
    return pl.BlockSpec(shape, lambda *_: (0,) * nd, pipeline_mode=pl.Buffered(1))


def _even_in_kernel(x_ref, g_ref, w_ref, hm_ref, qg_ref, kg_ref,
                    a_ref, q_ref, k_ref, v_ref, h_ref):
    cw = a_ref.shape[-1]
    x = x_ref[...]
    h_ref[...] = _rms(x, g_ref[...]).astype(BF16)

    a_lin = _dot(h_ref[...], w_ref[:, 0:cw])
    a_gate = _dot(h_ref[...], w_ref[:, cw:2 * cw])
    a_ref[...] = a_lin * jax.nn.sigmoid(a_gate)

    def head_norm(z, gain):
        s = z * z
        hi = s.astype(BF16)
        lo = (s - hi.astype(F32)).astype(BF16)
        ss = _dot(hi, hm_ref[...]) + _dot(lo, hm_ref[...])
        return z * lax.rsqrt(ss * (1.0 / HEAD_DIM) + EPS) * gain

    base = 2 * cw
    q = _dot(h_ref[...], w_ref[:, base:base + ATT_WIDTH])
    q_ref[...] = (head_norm(q, qg_ref[...]) * (HEAD_DIM ** -0.5 * LOG2E)).astype(BF16)
    k = _dot(h_ref[...], w_ref[:, base + ATT_WIDTH:base + 2 * ATT_WIDTH])
    k_ref[...] = head_norm(k, kg_ref[...]).astype(BF16)
    v = _dot(h_ref[...], w_ref[:, base + 2 * ATT_WIDTH:base + 3 * ATT_WIDTH])
    v_ref[...] = v.astype(BF16)


def _even_in(x2, g, w_bf, head_ones, qg, kg):
    t, d = x2.shape
    n = w_bf.shape[1]
    cw = (n - 3 * ATT_WIDTH) // 2
    tm = ROW_TILE
    row = lambda width: pl.BlockSpec((tm, width), lambda i: (i, 0))
    limit = _vmem_limit(
        [_nbytes((tm, d), F32), _nbytes((tm, cw), F32)] + 3 * [_nbytes((tm, ATT_WIDTH), BF16)],
        [_nbytes(w_bf.shape, BF16), _nbytes(head_ones.shape, BF16)],
        [_nbytes((tm, d), BF16), 4 * _nbytes((tm, 2 * cw), F32)])
    return pl.pallas_call(
        _even_in_kernel,
        grid=(t // tm,),
        in_specs=[row(d), _resident((1, d)), _resident(w_bf.shape), _resident(head_ones.shape),
                  _resident((1, ATT_WIDTH)), _resident((1, ATT_WIDTH))],
        out_specs=[row(cw), row(ATT_WIDTH), row(ATT_WIDTH), row(ATT_WIDTH)],
        out_shape=[jax.ShapeDtypeStruct((t, cw), F32)] +
                  3 * [jax.ShapeDtypeStruct((t, ATT_WIDTH), BF16)],
        scratch_shapes=[pltpu.VMEM((tm, d), BF16)],
        compiler_params=pltpu.CompilerParams(
            dimension_semantics=("parallel",), vmem_limit_bytes=limit),
        name="even_in",
    )(x2, g, w_bf, head_ones, qg, kg)


def _conv_module_kernel(prev_ref, cur_ref, w_ref, b_ref, lg_ref, lb_ref, o_ref, buf_ref, sh_ref):
    ts = cur_ref.shape[0]
    halo = prev_ref.shape[0]
    taps = w_ref.shape[0]
    span = sh_ref.shape[1]
    first = pl.program_id(1) == 0
    buf_ref[0:halo, :] = jnp.where(first, 0.0, prev_ref[...])
    buf_ref[halo:halo + ts, :] = cur_ref[...]
    for s in range(1, SUBLANES):
        for r in range(0, span, CONV_SUB):
            n = min(CONV_SUB, span - r)
            sh_ref[s - 1, r:r + n, :] = buf_ref[r + s:r + s + n, :]
    off = halo - (taps - 1)
    taps_w = [jnp.concatenate([w_ref[k]] * (CONV_SUB // SUBLANES), axis=0) for k in range(taps)]
    for r in range(0, ts, CONV_SUB):
        acc = jnp.zeros((CONV_SUB, cur_ref.shape[1]), F32) + b_ref[...]
        for k in range(taps):
            s = (off + k) % SUBLANES
            lo = r + off + k - s
            rows = buf_ref[lo:lo + CONV_SUB, :] if s == 0 else sh_ref[s - 1, lo:lo + CONV_SUB, :]
            acc = acc + taps_w[k] * rows
        mu = jnp.mean(acc, axis=-1, keepdims=True)
        cen = acc - mu
        var = jnp.mean(cen * cen, axis=-1, keepdims=True)
        y = cen * lax.rsqrt(var + EPS) * lg_ref[...] + lb_ref[...]
        o_ref[r:r + CONV_SUB, :] = (y * jax.nn.sigmoid(y)).astype(o_ref.dtype)


def _conv_module(a2, batch, conv_w, conv_b, ln_g, ln_b):
    t, c = a2.shape
    ts = ROW_TILE
    nt = t // batch // ts
    per = ts // CONV_HALO
    shifted = (SUBLANES - 1, ts + CONV_HALO - SUBLANES, c)
    limit = _vmem_limit([_nbytes((ts, c), F32), _nbytes((CONV_HALO, c), F32),
                         _nbytes((ts, c), BF16)],
                        [_nbytes(conv_w.shape, F32)],
                        [_nbytes((ts + CONV_HALO, c), F32), _nbytes(shifted, F32)])
    return pl.pallas_call(
        _conv_module_kernel,
        grid=(batch, nt),
        in_specs=[
            pl.BlockSpec((CONV_HALO, c), lambda b, i: (jnp.maximum((b * nt + i) * per - 1, 0), 0)),
            pl.BlockSpec((ts, c), lambda b, i: (b * nt + i, 0)),
            _resident(conv_w.shape), _resident((1, c)), _resident((1, c)), _resident((1, c))],
        out_specs=pl.BlockSpec((ts, c), lambda b, i: (b * nt + i, 0)),
        out_shape=jax.ShapeDtypeStruct((t, c), BF16),
        scratch_shapes=[pltpu.VMEM((ts + CONV_HALO, c), F32), pltpu.VMEM(shifted, F32)],
        compiler_params=pltpu.CompilerParams(
            dimension_semantics=("parallel", "parallel"), vmem_limit_bytes=limit),
        name="conv_module",
    )(a2, a2, conv_w, conv_b, ln_g, ln_b)


def _attn_kernel(q_ref, k0_ref, k1_ref, k2_ref, v0_ref, v1_ref, v2_ref, bias_ref, o_ref):
    low = lax.broadcasted_iota(jnp.int32, (1, 2 * HEAD_DIM), 1) < HEAD_DIM
    keep = [jnp.where(low, 1.0, 0.0).astype(BF16), jnp.where(low, 0.0, 1.0).astype(BF16)]
    for p in range(ATT_HEADS // 2):
        cols = slice(2 * HEAD_DIM * p, 2 * HEAD_DIM * (p + 1))
        qp = q_ref[:, cols]
        kp = jnp.concatenate([k0_ref[:, cols], k1_ref[:, cols], k2_ref[:, cols]], axis=0)
        vp = jnp.concatenate([v0_ref[:, cols], v1_ref[:, cols], v2_ref[:, cols]], axis=0)
        tq = qp.shape[0]
        q2 = jnp.concatenate([qp * keep[0], qp * keep[1]], axis=0)
        s = _dot_nt(q2, kp) + bias_ref[0, 2 * p:2 * p + 2].reshape(2 * tq, kp.shape[0])
        m = jnp.max(s, axis=-1, keepdims=True)
        pe = jnp.exp2(s - m)
        den = jnp.sum(pe, axis=-1, keepdims=True)
        o2 = _dot(pe.astype(BF16), vp) * (1.0 / den)
        o_ref[:, cols] = jnp.where(low, o2[0:tq], o2[tq:2 * tq]).astype(o_ref.dtype)


def _attn_bias_table(rel_bias):
    t = np.arange(ATT_TILE)[:, None]
    kk = np.arange(ATT_KEYS)[None, :]
    rel_chunk = kk // CHUNK - t // CHUNK
    band = (rel_chunk >= 0) & (rel_chunk <= N_LEFT_CHUNKS)
    variants = np.stack([band & (kk >= (2 - v) * ATT_TILE) for v in range(3)])
    period = ATT_TILE + ATT_KEYS
    delta = np.arange(period)
    delta = np.where(delta < ATT_KEYS, delta, delta - period)
    idx = np.clip(2 * ATT_TILE - delta, -(CHUNK - 1), REL_MAX) + (CHUNK - 1)
    one_period = rel_bias[:, idx].astype(F32)
    heads = rel_bias.shape[0]
    flat = jnp.tile(one_period, (1, ATT_TILE))[:, :ATT_TILE * (period - 1)]
    toeplitz = flat.reshape(heads, ATT_TILE, period - 1)[:, :, :ATT_KEYS]
    return jnp.where(variants[:, None], toeplitz[None], NEG_INF) * LOG2E


def _attention(q2, k2, v2, batch, bias_tab):
    t, w = q2.shape
    tq = ATT_TILE
    nt = t // batch // tq
    blk = lambda back: pl.BlockSpec(
        (tq, w), lambda b, i: (b * nt + jnp.maximum(i - back, 0), 0))
    limit = _vmem_limit(8 * [_nbytes((tq, w), BF16)] + [_nbytes(bias_tab.shape[1:], F32)],
                        [], [6 * _nbytes((tq, ATT_KEYS), F32)])
    return pl.pallas_call(
        _attn_kernel,
        grid=(batch, nt),
        in_specs=[blk(0), blk(2), blk(1), blk(0), blk(2), blk(1), blk(0),
                  pl.BlockSpec((1,) + bias_tab.shape[1:],
                               lambda b, i: (jnp.minimum(i, 2), 0, 0, 0))],
        out_specs=blk(0),
        out_shape=jax.ShapeDtypeStruct((t, w), BF16),
        compiler_params=pltpu.CompilerParams(
            dimension_semantics=("parallel", "parallel"), vmem_limit_bytes=limit),
        name="chunk_attention",
    )(q2, k2, k2, k2, v2, v2, v2, bias_tab)


def _gmlp_in_kernel(x_ref, g_ref, w_ref, lg_ref, lb_ref, wsp_ref, bsp_ref, y_ref,
                    h_ref, v_ref, vn_ref):
    tm = x_ref.shape[0]
    width = v_ref.shape[1]
    gd = width // GMLP_GROUPS
    h_ref[...] = _rms(x_ref[...], g_ref[...]).astype(BF16)

    total = jnp.zeros((tm, 1), F32)
    for g in range(GMLP_GROUPS):
        zv = _gelu_tanh(_dot(h_ref[...], w_ref[:, width + g * gd:width + (g + 1) * gd]))
        v_ref[:, g * gd:(g + 1) * gd] = zv
        total = total + jnp.sum(zv, axis=-1, keepdims=True)
    mu = total * (1.0 / width)
    sq = jnp.zeros((tm, 1), F32)
    for g in range(GMLP_GROUPS):
        cen = v_ref[:, g * gd:(g + 1) * gd] - mu
        sq = sq + jnp.sum(cen * cen, axis=-1, keepdims=True)
    rstd = lax.rsqrt(sq * (1.0 / width) + EPS)
    for g in range(GMLP_GROUPS):
        sl = slice(g * gd, (g + 1) * gd)
        vn_ref[:, sl] = ((v_ref[:, sl] - mu) * rstd * lg_ref[:, sl] + lb_ref[:, sl]).astype(BF16)

    row = lax.broadcasted_iota(jnp.int32, (GMLP_CHUNK, GMLP_CHUNK), 0)
    col = lax.broadcasted_iota(jnp.int32, (GMLP_CHUNK, GMLP_CHUNK), 1)
    causal = row >= col
    for g in range(GMLP_GROUPS):
        sl = slice(g * gd, (g + 1) * gd)
        u = _gelu_tanh(_dot(h_ref[...], w_ref[:, sl]))
        wm = jnp.where(causal, wsp_ref[g], 0.0).astype(BF16)
        bias = jnp.concatenate([bsp_ref[g]] * (gd // LANES), axis=-1)
        for r in range(0, tm, GMLP_CHUNK):
            sv = _dot(wm, vn_ref[r:r + GMLP_CHUNK, sl]) + bias
            y_ref[r:r + GMLP_CHUNK, sl] = (u[r:r + GMLP_CHUNK] * sv).astype(y_ref.dtype)


def _gmlp_in(x2, g, w_bf, ln_g, ln_b, w_sp, b_sp_tile):
    t, d = x2.shape
    width = w_bf.shape[1] // 2
    tm = ROW_TILE
    limit = _vmem_limit(
        [_nbytes((tm, d), F32), _nbytes((tm, width), BF16)],
        [_nbytes(w_bf.shape, BF16), _nbytes(w_sp.shape, F32), _nbytes(b_sp_tile.shape, F32)],
        [_nbytes((tm, d), BF16), _nbytes((tm, width), F32), _nbytes((tm, width), BF16),
         8 * _nbytes((tm, width // GMLP_GROUPS), F32)])
    return pl.pallas_call(
        _gmlp_in_kernel,
        grid=(t // tm,),
        in_specs=[pl.BlockSpec((tm, d), lambda i: (i, 0)), _resident((1, d)),
                  _resident(w_bf.shape), _resident((1, width)), _resident((1, width)),
                  _resident(w_sp.shape), _resident(b_sp_tile.shape)],
        out_specs=pl.BlockSpec((tm, width), lambda i: (i, 0)),
        out_shape=jax.ShapeDtypeStruct((t, width), BF16),
        scratch_shapes=[pltpu.VMEM((tm, d), BF16), pltpu.VMEM((tm, width), F32),
                        pltpu.VMEM((tm, width), BF16)],
        compiler_params=pltpu.CompilerParams(
            dimension_semantics=("parallel",), vmem_limit_bytes=limit),
        name="gmlp_in",
    )(x2, g, w_bf, ln_g, ln_b, w_sp, b_sp_tile)


def _mix_ffn_kernel(n_mix, *refs):
    x_ref = refs[0]
    y_refs = refs[1:1 + n_mix]
    wm_refs = refs[1 + n_mix:1 + 2 * n_mix]
    (g_ref, wg_ref, wu_ref, dww_ref, dwb_ref, wd_ref,
     o_ref, h_ref, act_ref, gbuf_ref, carry_ref) = refs[1 + 2 * n_mix:]
    tm = x_ref.shape[0]
    d_ff = wg_ref.shape[1]
    fc = gbuf_ref.shape[1]

    @pl.when(pl.program_id(1) == 0)
    def _():
        carry_ref[...] = jnp.zeros_like(carry_ref)

    x1 = x_ref[...]
    for y_ref, wm_ref in zip(y_refs, wm_refs):
        x1 = x1 + _dot(y_ref[...], wm_ref[...])
    o_ref[...] = x1
    h_ref[...] = _rms(x1, g_ref[...]).astype(BF16)

    for c in range(0, d_ff, fc):
        cols = slice(c, c + fc)
        gate = _dot(h_ref[...], wg_ref[:, cols])
        up = _dot(h_ref[...], wu_ref[:, cols])
        gbuf_ref[0:SUBLANES, :] = carry_ref[:, cols]
        gbuf_ref[SUBLANES:SUBLANES + tm, :] = gate
        carry_ref[:, cols] = gate[tm - SUBLANES:tm, :]
        conv = dwb_ref[:, cols]
        for k in range(FFN_TAPS):
            lo = SUBLANES - (FFN_TAPS - 1) + k
            conv = conv + dww_ref[k:k + 1, cols] * gbuf_ref[lo:lo + tm, :]
        act_ref[:, cols] = (conv * jax.nn.sigmoid(conv) * up).astype(BF16)

    o_ref[...] = o_ref[...] + _dot(act_ref[...], wd_ref[...])


def _mix_ffn(x2, batch, mix_pairs, g, wg, wu, dww, dwb, wd):
    t, d = x2.shape
    tm = ROW_TILE
    nt = t // batch // tm
    d_ff = wg.shape[1]
    fc = FF_CHUNK
    n_mix = len(mix_pairs)
    row = lambda width: pl.BlockSpec((tm, width), lambda b, i: (b * nt + i, 0))
    ys = [y for y, _ in mix_pairs]
    wms = [w for _, w in mix_pairs]
    limit = _vmem_limit(
        2 * [_nbytes((tm, d), F32)] + [_nbytes((tm, y.shape[1]), BF16) for y in ys],
        [_nbytes(w.shape, BF16) for w in wms] + 3 * [_nbytes(wg.shape, BF16)] +
        [_nbytes((16, d_ff), F32)],
        [_nbytes((tm, d), BF16), _nbytes((tm, d_ff), BF16),
         _nbytes((tm + SUBLANES, fc), F32), _nbytes((SUBLANES, d_ff), F32),
         4 * _nbytes((tm, fc), F32)])
    return pl.pallas_call(
        functools.partial(_mix_ffn_kernel, n_mix),
        grid=(batch, nt),
        in_specs=[row(d)] + [row(y.shape[1]) for y in ys] + [_resident(w.shape) for w in wms] +
                 [_resident((1, d)), _resident(wg.shape), _resident(wu.shape),
                  _resident(dww.shape), _resident(dwb.shape), _resident(wd.shape)],
        out_specs=row(d),
        out_shape=jax.ShapeDtypeStruct((t, d), F32),
        scratch_shapes=[pltpu.VMEM((tm, d), BF16), pltpu.VMEM((tm, d_ff), BF16),
                        pltpu.VMEM((tm + SUBLANES, fc), F32),
                        pltpu.VMEM((SUBLANES, d_ff), F32)],
        compiler_params=pltpu.CompilerParams(
            dimension_semantics=("arbitrary", "arbitrary"), vmem_limit_bytes=limit),
        name="mix_ffn",
    )(x2, *ys, *wms, g, wg, wu, dww, dwb, wd)


def kernel(x, mix_norm_g, w_in_e, conv_w, conv_b, conv_ln_g, conv_ln_b, qn_g, kn_g, rel_bias,
           w_out_e, w_in_o, gate_ln_g, gate_ln_b, w_sp, b_sp, w_out_o, ffn_norm_g, w_gate, w_up,
           ffn_dw_w, ffn_dw_b, w_down):
    batch, seq, d = x.shape
    depth = mix_norm_g.shape[0]
    d_ff = w_gate.shape[-1]
    assert seq % ROW_TILE == 0 and seq % ATT_TILE == 0 and d_ff % FF_CHUNK == 0
    assert ROW_TILE % GMLP_CHUNK == 0 and ROW_TILE % CONV_HALO == 0 and ATT_TILE % CHUNK == 0
    assert ATT_KEYS >= ATT_TILE + N_LEFT_CHUNKS * CHUNK and CONV_HALO >= CONV_TAPS - 1

    x2 = x.reshape(batch * seq, d)
    head_ones = jnp.asarray(np.kron(np.eye(ATT_HEADS), np.ones((HEAD_DIM, HEAD_DIM))), BF16)
    row = lambda vec: vec.reshape(1, -1).astype(F32)

    for layer in range(depth):
        i = layer // 2
        if layer % 2 == 0:
            a, q, k, v = _even_in(
                x2, row(mix_norm_g[layer]), w_in_e[i].astype(BF16), head_ones,
                row(jnp.tile(qn_g[i], ATT_HEADS)), row(jnp.tile(kn_g[i], ATT_HEADS)))
            taps_w = jnp.broadcast_to(conv_w[i].astype(F32)[:, None, :],
                                      (conv_w.shape[1], SUBLANES, conv_w.shape[2]))
            a = _conv_module(a, batch, taps_w, row(conv_b[i]),
                             row(conv_ln_g[i]), row(conv_ln_b[i]))
            o = _attention(q, k, v, batch, _attn_bias_table(rel_bias[i]))
            cw = a.shape[1]
            w_out = w_out_e[i].astype(BF16)
            mix_pairs = [(a, w_out[:cw]), (o, w_out[cw:])]
        else:
            b_tile = jnp.broadcast_to(b_sp[i].astype(F32)[:, :, None],
                                      (GMLP_GROUPS, GMLP_CHUNK, LANES))
            y = _gmlp_in(x2, row(mix_norm_g[layer]), w_in_o[i].astype(BF16),
                         row(gate_ln_g[i]), row(gate_ln_b[i]), w_sp[i].astype(F32), b_tile)
            mix_pairs = [(y, w_out_o[i].astype(BF16))]
        x2 = _mix_ffn(
            x2, batch, mix_pairs, row(ffn_norm_g[layer]),
            w_gate[layer].astype(BF16), w_up[layer].astype(BF16),
            ffn_dw_w[layer].astype(F32), row(ffn_dw_b[layer]), w_down[layer].astype(BF16))
    return x2.reshape(batch, seq, d)
```

```python
import functools

import numpy as np
import jax
import jax.numpy as jnp
from jax import lax
from jax.experimental import pallas as pl
from jax.experimental.pallas import tpu as pltpu

F32 = jnp.float32
BF16 = jnp.bfloat16

EPS = 1e-6
CHUNK = 64
N_LEFT_CHUNKS = 8
REL_MAX = 4 * CHUNK
ATT_HEADS = 8
HEAD_DIM = 64
ATT_WIDTH = ATT_HEADS * HEAD_DIM
CONV_TAPS = 31
GMLP_CHUNK = 128
GMLP_GROUPS = 8
FFN_TAPS = 3
NEG_INF = -1e30
LOG2E = 1.4426950408889634

SUBLANES = 8
LANES = 128
VMEM_LIMIT_CAP_BYTES = 60000 * 1024

ROW_TILE = 512
EVEN_TILE = 1024
CONV_TILE = 1024
GMLP_TILE = 1024
ATT_TILE = 256
ATT_KEYS = 3 * ATT_TILE
ATT_SUBS = 2
FF_CHUNK = 256
CONV_HALO = 32
CONV_SUB = 32


def _dot(a, b):
    return jnp.dot(a, b, preferred_element_type=F32)


def _dot_nt(a, b):
    return lax.dot_general(a, b, (((1,), (1,)), ((), ())), preferred_element_type=F32)


def _rms(x, g):
    return x * lax.rsqrt(jnp.mean(x * x, axis=-1, keepdims=True) + EPS) * g


def _gelu_tanh(x):
    a = float(-2.0 * np.sqrt(2.0 / np.pi) * LOG2E)
    b = a * 0.044715
    return x * (1.0 / (1.0 + jnp.exp2(x * (b * (x * x) + a))))


def _nbytes(shape, dtype):
    return int(np.prod(shape)) * jnp.dtype(dtype).itemsize


def _vmem_limit(pipelined, resident, scratch):
    need = 2 * sum(pipelined) + sum(resident) + sum(scratch)
    need += 4 * max(pipelined)
    return int(min(VMEM_LIMIT_CAP_BYTES, max(need, 16 * 1024 * 1024)))


def _resident(shape):
    nd = len(shape)
    return pl.BlockSpec(shape, lambda *_: (0,) * nd, pipeline_mode=pl.Buffered(1))


def _even_in_kernel(x_ref, g_ref, w_ref, hm_ref, qg_ref, kg_ref,
                    a_ref, q_ref, k_ref, v_ref, h_ref):
    cw = a_ref.shape[-1]
    x = x_ref[...]
    h_ref[...] = _rms(x, g_ref[...]).astype(BF16)

    a_lin = _dot(h_ref[...], w_ref[:, 0:cw])
    a_gate = _dot(h_ref[...], w_ref[:, cw:2 * cw])
    a_ref[...] = a_lin * jax.nn.sigmoid(a_gate)

    def head_norm(z, gain):
        s = z * z
        hi = s.astype(BF16)
        lo = (s - hi.astype(F32)).astype(BF16)
        ss = _dot(hi, hm_ref[...]) + _dot(lo, hm_ref[...])
        return z * lax.rsqrt(ss * (1.0 / HEAD_DIM) + EPS) * gain

    base = 2 * cw
    q = _dot(h_ref[...], w_ref[:, base:base + ATT_WIDTH])
    q_ref[...] = (head_norm(q, qg_ref[...]) * (HEAD_DIM ** -0.5 * LOG2E)).astype(BF16)
    k = _dot(h_ref[...], w_ref[:, base + ATT_WIDTH:base + 2 * ATT_WIDTH])
    k_ref[...] = head_norm(k, kg_ref[...]).astype(BF16)
    v = _dot(h_ref[...], w_ref[:, base + 2 * ATT_WIDTH:base + 3 * ATT_WIDTH])
    v_ref[...] = v.astype(BF16)


def _even_in(x2, g, w_bf, head_ones, qg, kg):
    t, d = x2.shape
    n = w_bf.shape[1]
    cw = (n - 3 * ATT_WIDTH) // 2
    tm = EVEN_TILE
    row = lambda width: pl.BlockSpec((tm, width), lambda i: (i, 0))
    limit = _vmem_limit(
        [_nbytes((tm, d), F32), _nbytes((tm, cw), F32)] + 3 * [_nbytes((tm, ATT_WIDTH), BF16)],
        [_nbytes(w_bf.shape, BF16), _nbytes(head_ones.shape, BF16)],
        [_nbytes((tm, d), BF16), 4 * _nbytes((tm, 2 * cw), F32)])
    return pl.pallas_call(
        _even_in_kernel,
        grid=(t // tm,),
        in_specs=[row(d), _resident((1, d)), _resident(w_bf.shape), _resident(head_ones.shape),
                  _resident((1, ATT_WIDTH)), _resident((1, ATT_WIDTH))],
        out_specs=[row(cw), row(ATT_WIDTH), row(ATT_WIDTH), row(ATT_WIDTH)],
        out_shape=[jax.ShapeDtypeStruct((t, cw), F32)] +
                  3 * [jax.ShapeDtypeStruct((t, ATT_WIDTH), BF16)],
        scratch_shapes=[pltpu.VMEM((tm, d), BF16)],
        compiler_params=pltpu.CompilerParams(
            dimension_semantics=("parallel",), vmem_limit_bytes=limit),
        name="even_in",
    )(x2, g, w_bf, head_ones, qg, kg)


def _conv_module_kernel(prev_ref, cur_ref, w_ref, b_ref, lg_ref, lb_ref, o_ref, buf_ref, sh_ref):
    ts = cur_ref.shape[0]
    halo = prev_ref.shape[0]
    taps = w_ref.shape[0]
    span = sh_ref.shape[1]
    first = pl.program_id(1) == 0
    buf_ref[0:halo, :] = jnp.where(first, 0.0, prev_ref[...])
    buf_ref[halo:halo + ts, :] = cur_ref[...]
    for s in range(1, SUBLANES):
        for r in range(0, span, CONV_SUB):
            n = min(CONV_SUB, span - r)
            sh_ref[s - 1, r:r + n, :] = buf_ref[r + s:r + s + n, :]
    off = halo - (taps - 1)
    taps_w = [jnp.concatenate([w_ref[k]] * (CONV_SUB // SUBLANES), axis=0) for k in range(taps)]
    for r in range(0, ts, CONV_SUB):
        acc = jnp.zeros((CONV_SUB, cur_ref.shape[1]), F32) + b_ref[...]
        for k in range(taps):
            s = (off + k) % SUBLANES
            lo = r + off + k - s
            rows = buf_ref[lo:lo + CONV_SUB, :] if s == 0 else sh_ref[s - 1, lo:lo + CONV_SUB, :]
            acc = acc + taps_w[k] * rows
        mu = jnp.mean(acc, axis=-1, keepdims=True)
        cen = acc - mu
        var = jnp.mean(cen * cen, axis=-1, keepdims=True)
        y = cen * lax.rsqrt(var + EPS) * lg_ref[...] + lb_ref[...]
        o_ref[r:r + CONV_SUB, :] = (y * jax.nn.sigmoid(y)).astype(o_ref.dtype)


def _conv_module(a2, batch, conv_w, conv_b, ln_g, ln_b):
    t, c = a2.shape
    ts = CONV_TILE
    nt = t // batch // ts
    per = ts // CONV_HALO
    shifted = (SUBLANES - 1, ts + CONV_HALO - SUBLANES, c)
    limit = _vmem_limit([_nbytes((ts, c), F32), _nbytes((CONV_HALO, c), F32),
                         _nbytes((ts, c), BF16)],
                        [_nbytes(conv_w.shape, F32)],
                        [_nbytes((ts + CONV_HALO, c), F32), _nbytes(shifted, F32)])
    return pl.pallas_call(
        _conv_module_kernel,
        grid=(batch, nt),
        in_specs=[
            pl.BlockSpec((CONV_HALO, c), lambda b, i: (jnp.maximum((b * nt + i) * per - 1, 0), 0)),
            pl.BlockSpec((ts, c), lambda b, i: (b * nt + i, 0)),
            _resident(conv_w.shape), _resident((1, c)), _resident((1, c)), _resident((1, c))],
        out_specs=pl.BlockSpec((ts, c), lambda b, i: (b * nt + i, 0)),
        out_shape=jax.ShapeDtypeStruct((t, c), BF16),
        scratch_shapes=[pltpu.VMEM((ts + CONV_HALO, c), F32), pltpu.VMEM(shifted, F32)],
        compiler_params=pltpu.CompilerParams(
            dimension_semantics=("parallel", "parallel"), vmem_limit_bytes=limit),
        name="conv_module",
    )(a2, a2, conv_w, conv_b, ln_g, ln_b)


def _attn_kernel(*refs):
    n_kv = ATT_SUBS + 2
    q_ref, k_refs, v_refs = refs[0], refs[1:1 + n_kv], refs[1 + n_kv:1 + 2 * n_kv]
    bias_refs, o_ref = refs[1 + 2 * n_kv:-1], refs[-1]
    low = lax.broadcasted_iota(jnp.int32, (1, 2 * HEAD_DIM), 1) < HEAD_DIM
    keep = [jnp.where(low, 1.0, 0.0).astype(BF16), jnp.where(low, 0.0, 1.0).astype(BF16)]
    tq = ATT_TILE
    for j in range(ATT_SUBS):
        rows = slice(j * tq, (j + 1) * tq)
        for p in range(ATT_HEADS // 2):
            cols = slice(2 * HEAD_DIM * p, 2 * HEAD_DIM * (p + 1))
            qp = q_ref[rows, cols]
            kp = jnp.concatenate([r[:, cols] for r in k_refs[j:j + 3]], axis=0)
            vp = jnp.concatenate([r[:, cols] for r in v_refs[j:j + 3]], axis=0)
            q2 = jnp.concatenate([qp * keep[0], qp * keep[1]], axis=0)
            bias = bias_refs[j][0, 2 * p:2 * p + 2].reshape(2 * tq, kp.shape[0])
            s = _dot_nt(q2, kp) + bias
            m = jnp.max(s, axis=-1, keepdims=True)
            pe = jnp.exp2(s - m)
            den = jnp.sum(pe, axis=-1, keepdims=True)
            o2 = _dot(pe.astype(BF16), vp) * (1.0 / den)
            o_ref[rows, cols] = jnp.where(low, o2[0:tq], o2[tq:2 * tq]).astype(o_ref.dtype)


def _attn_bias_table(rel_bias):
    t = np.arange(ATT_TILE)[:, None]
    kk = np.arange(ATT_KEYS)[None, :]
    rel_chunk = kk // CHUNK - t // CHUNK
    band = (rel_chunk >= 0) & (rel_chunk <= N_LEFT_CHUNKS)
    variants = np.stack([band & (kk >= (2 - v) * ATT_TILE) for v in range(3)])
    period = ATT_TILE + ATT_KEYS
    delta = np.arange(period)
    delta = np.where(delta < ATT_KEYS, delta, delta - period)
    idx = np.clip(2 * ATT_TILE - delta, -(CHUNK - 1), REL_MAX) + (CHUNK - 1)
    one_period = rel_bias[:, idx].astype(F32)
    heads = rel_bias.shape[0]
    flat = jnp.tile(one_period, (1, ATT_TILE))[:, :ATT_TILE * (period - 1)]
    toeplitz = flat.reshape(heads, ATT_TILE, period - 1)[:, :, :ATT_KEYS]
    return jnp.where(variants[:, None], toeplitz[None], NEG_INF) * LOG2E


def _attention(q2, k2, v2, batch, bias_tab):
    t, w = q2.shape
    tq = ATT_TILE
    step = ATT_SUBS * tq
    nt = t // batch // step
    n_kv = ATT_SUBS + 2
    q_blk = pl.BlockSpec((step, w), lambda b, i: (b * nt + i, 0))
    kv_blk = lambda m: pl.BlockSpec(
        (tq, w), lambda b, i: (b * nt * ATT_SUBS + jnp.maximum(ATT_SUBS * i - 2 + m, 0), 0))
    bias_blk = lambda j: pl.BlockSpec(
        (1,) + bias_tab.shape[1:], lambda b, i: (jnp.minimum(ATT_SUBS * i + j, 2), 0, 0, 0),
        pipeline_mode=pl.Buffered(1))
    limit = _vmem_limit([2 * _nbytes((step, w), BF16)] + 2 * n_kv * [_nbytes((tq, w), BF16)],
                        ATT_SUBS * [_nbytes(bias_tab.shape[1:], F32)],
                        [8 * _nbytes((2 * tq, ATT_KEYS), F32)])
    return pl.pallas_call(
        _attn_kernel,
        grid=(batch, nt),
        in_specs=[q_blk] + 2 * [kv_blk(m) for m in range(n_kv)] +
                 [bias_blk(j) for j in range(ATT_SUBS)],
        out_specs=q_blk,
        out_shape=jax.ShapeDtypeStruct((t, w), BF16),
        compiler_params=pltpu.CompilerParams(
            dimension_semantics=("parallel", "parallel"), vmem_limit_bytes=limit),
        name="chunk_attention",
    )(q2, *(n_kv * [k2]), *(n_kv * [v2]), *(ATT_SUBS * [bias_tab]))


def _gmlp_in_kernel(x_ref, g_ref, w_ref, lg_ref, lb_ref, wsp_ref, bsp_ref, y_ref,
                    h_ref, v_ref, vn_ref):
    tm = x_ref.shape[0]
    width = v_ref.shape[1]
    gd = width // GMLP_GROUPS
    h_ref[...] = _rms(x_ref[...], g_ref[...]).astype(BF16)

    total = jnp.zeros((tm, 1), F32)
    for g in range(GMLP_GROUPS):
        zv = _gelu_tanh(_dot(h_ref[...], w_ref[:, width + g * gd:width + (g + 1) * gd]))
        v_ref[:, g * gd:(g + 1) * gd] = zv
        total = total + jnp.sum(zv, axis=-1, keepdims=True)
    mu = total * (1.0 / width)
    sq = jnp.zeros((tm, 1), F32)
    for g in range(GMLP_GROUPS):
        cen = v_ref[:, g * gd:(g + 1) * gd] - mu
        sq = sq + jnp.sum(cen * cen, axis=-1, keepdims=True)
    rstd = lax.rsqrt(sq * (1.0 / width) + EPS)
    for g in range(GMLP_GROUPS):
        sl = slice(g * gd, (g + 1) * gd)
        vn_ref[:, sl] = ((v_ref[:, sl] - mu) * rstd * lg_ref[:, sl] + lb_ref[:, sl]).astype(BF16)

    row = lax.broadcasted_iota(jnp.int32, (GMLP_CHUNK, GMLP_CHUNK), 0)
    col = lax.broadcasted_iota(jnp.int32, (GMLP_CHUNK, GMLP_CHUNK), 1)
    causal = row >= col
    for g in range(GMLP_GROUPS):
        sl = slice(g * gd, (g + 1) * gd)
        u = _gelu_tanh(_dot(h_ref[...], w_ref[:, sl]))
        wm = jnp.where(causal, wsp_ref[g], 0.0).astype(BF16)
        bias = jnp.concatenate([bsp_ref[g]] * (gd // LANES), axis=-1)
        for r in range(0, tm, GMLP_CHUNK):
            sv = _dot(wm, vn_ref[r:r + GMLP_CHUNK, sl]) + bias
            y_ref[r:r + GMLP_CHUNK, sl] = (u[r:r + GMLP_CHUNK] * sv).astype(y_ref.dtype)


def _gmlp_in(x2, g, w_bf, ln_g, ln_b, w_sp, b_sp_tile):
    t, d = x2.shape
    width = w_bf.shape[1] // 2
    tm = GMLP_TILE
    limit = _vmem_limit(
        [_nbytes((tm, d), F32), _nbytes((tm, width), BF16)],
        [_nbytes(w_bf.shape, BF16), _nbytes(w_sp.shape, F32), _nbytes(b_sp_tile.shape, F32)],
        [_nbytes((tm, d), BF16), _nbytes((tm, width), F32), _nbytes((tm, width), BF16),
         8 * _nbytes((tm, width // GMLP_GROUPS), F32)])
    return pl.pallas_call(
        _gmlp_in_kernel,
        grid=(t // tm,),
        in_specs=[pl.BlockSpec((tm, d), lambda i: (i, 0)), _resident((1, d)),
                  _resident(w_bf.shape), _resident((1, width)), _resident((1, width)),
                  _resident(w_sp.shape), _resident(b_sp_tile.shape)],
        out_specs=pl.BlockSpec((tm, width), lambda i: (i, 0)),
        out_shape=jax.ShapeDtypeStruct((t, width), BF16),
        scratch_shapes=[pltpu.VMEM((tm, d), BF16), pltpu.VMEM((tm, width), F32),
                        pltpu.VMEM((tm, width), BF16)],
        compiler_params=pltpu.CompilerParams(
            dimension_semantics=("parallel",), vmem_limit_bytes=limit),
        name="gmlp_in",
    )(x2, g, w_bf, ln_g, ln_b, w_sp, b_sp_tile)


def _mix_ffn_kernel(n_mix, *refs):
    x_ref = refs[0]
    y_refs = refs[1:1 + n_mix]
    wm_refs = refs[1 + n_mix:1 + 2 * n_mix]
    (g_ref, wg_ref, wu_ref, dww_ref, dwb_ref, wd_ref,
     o_ref, h_ref, act_ref, carry_ref) = refs[1 + 2 * n_mix:]
    tm = x_ref.shape[0]
    d_ff = wg_ref.shape[1]
    fc = FF_CHUNK

    @pl.when(pl.program_id(1) == 0)
    def _():
        carry_ref[...] = jnp.zeros_like(carry_ref)

    x1 = x_ref[...]
    for y_ref, wm_ref in zip(y_refs, wm_refs):
        x1 = x1 + _dot(y_ref[...], wm_ref[...])
    o_ref[...] = x1
    h_ref[...] = _rms(x1, g_ref[...]).astype(BF16)

    for c in range(0, d_ff, fc):
        cols = slice(c, c + fc)
        gate = _dot(h_ref[...], wg_ref[:, cols])
        up = _dot(h_ref[...], wu_ref[:, cols])
        ext = jnp.concatenate([carry_ref[:, cols], gate], axis=0)
        carry_ref[:, cols] = gate[tm - SUBLANES:tm, :]
        conv = dwb_ref[:, cols] + dww_ref[FFN_TAPS - 1:FFN_TAPS, cols] * gate
        for k in range(FFN_TAPS - 2, -1, -1):
            ext = pltpu.roll(ext, 1, axis=0)
            conv = conv + dww_ref[k:k + 1, cols] * ext[SUBLANES:SUBLANES + tm]
        act_ref[:, cols] = (conv * jax.nn.sigmoid(conv) * up).astype(BF16)

    o_ref[...] = o_ref[...] + _dot(act_ref[...], wd_ref[...])


def _mix_ffn(x2, batch, mix_pairs, g, wg, wu, dww, dwb, wd):
    t, d = x2.shape
    tm = ROW_TILE
    nt = t // batch // tm
    d_ff = wg.shape[1]
    fc = FF_CHUNK
    n_mix = len(mix_pairs)
    row = lambda width: pl.BlockSpec((tm, width), lambda b, i: (b * nt + i, 0))
    ys = [y for y, _ in mix_pairs]
    wms = [w for _, w in mix_pairs]
    limit = _vmem_limit(
        2 * [_nbytes((tm, d), F32)] + [_nbytes((tm, y.shape[1]), BF16) for y in ys],
        [_nbytes(w.shape, BF16) for w in wms] + 3 * [_nbytes(wg.shape, BF16)] +
        [_nbytes((16, d_ff), F32)],
        [_nbytes((tm, d), BF16), _nbytes((tm, d_ff), BF16),
         _nbytes((tm + SUBLANES, fc), F32), _nbytes((SUBLANES, d_ff), F32),
         4 * _nbytes((tm, fc), F32)])
    return pl.pallas_call(
        functools.partial(_mix_ffn_kernel, n_mix),
        grid=(batch, nt),
        in_specs=[row(d)] + [row(y.shape[1]) for y in ys] + [_resident(w.shape) for w in wms] +
                 [_resident((1, d)), _resident(wg.shape), _resident(wu.shape),
                  _resident(dww.shape), _resident(dwb.shape), _resident(wd.shape)],
        out_specs=row(d),
        out_shape=jax.ShapeDtypeStruct((t, d), F32),
        scratch_shapes=[pltpu.VMEM((tm, d), BF16), pltpu.VMEM((tm, d_ff), BF16),
                        pltpu.VMEM((SUBLANES, d_ff), F32)],
        compiler_params=pltpu.CompilerParams(
            dimension_semantics=("arbitrary", "arbitrary"), vmem_limit_bytes=limit),
        name="mix_ffn",
    )(x2, *ys, *wms, g, wg, wu, dww, dwb, wd)


def kernel(x, mix_norm_g, w_in_e, conv_w, conv_b, conv_ln_g, conv_ln_b, qn_g, kn_g, rel_bias,
           w_out_e, w_in_o, gate_ln_g, gate_ln_b, w_sp, b_sp, w_out_o, ffn_norm_g, w_gate, w_up,
           ffn_dw_w, ffn_dw_b, w_down):
    batch, seq, d = x.shape
    depth = mix_norm_g.shape[0]
    d_ff = w_gate.shape[-1]
    assert seq % ROW_TILE == 0 and seq % (ATT_SUBS * ATT_TILE) == 0 and d_ff % FF_CHUNK == 0
    assert seq % GMLP_TILE == 0 and GMLP_TILE % GMLP_CHUNK == 0
    assert seq % EVEN_TILE == 0 and seq % CONV_TILE == 0
    assert CONV_TILE % CONV_HALO == 0 and CONV_TILE % CONV_SUB == 0 and ATT_TILE % CHUNK == 0
    assert ATT_KEYS >= ATT_TILE + N_LEFT_CHUNKS * CHUNK and CONV_HALO >= CONV_TAPS - 1

    x2 = x.reshape(batch * seq, d)
    head_ones = jnp.asarray(np.kron(np.eye(ATT_HEADS), np.ones((HEAD_DIM, HEAD_DIM))), BF16)
    row = lambda vec: vec.reshape(1, -1).astype(F32)

    for layer in range(depth):
        i = layer // 2
        if layer % 2 == 0:
            a, q, k, v = _even_in(
                x2, row(mix_norm_g[layer]), w_in_e[i].astype(BF16), head_ones,
                row(jnp.tile(qn_g[i], ATT_HEADS)), row(jnp.tile(kn_g[i], ATT_HEADS)))
            taps_w = jnp.broadcast_to(conv_w[i].astype(F32)[:, None, :],
                                      (conv_w.shape[1], SUBLANES, conv_w.shape[2]))
            a = _conv_module(a, batch, taps_w, row(conv_b[i]),
                             row(conv_ln_g[i]), row(conv_ln_b[i]))
            o = _attention(q, k, v, batch, _attn_bias_table(rel_bias[i]))
            cw = a.shape[1]
            w_out = w_out_e[i].astype(BF16)
            mix_pairs = [(a, w_out[:cw]), (o, w_out[cw:])]
        else:
            b_tile = jnp.broadcast_to(b_sp[i].astype(F32)[:, :, None],
                                      (GMLP_GROUPS, GMLP_CHUNK, LANES))
            y = _gmlp_in(x2, row(mix_norm_g[layer]), w_in_o[i].astype(BF16),
                         row(gate_ln_g[i]), row(gate_ln_b[i]), w_sp[i].astype(F32), b_tile)
            mix_pairs = [(y, w_out_o[i].astype(BF16))]
        x2 = _mix_ffn(
            x2, batch, mix_pairs, row(ffn_norm_g[layer]),
            w_gate[layer].astype(BF16), w_up[layer].astype(BF16),
            ffn_dw_w[layer].astype(F32), row(ffn_dw_b[layer]), w_down[layer].astype(BF16))
    return x2.reshape(batch, seq, d)
```

```python
import functools

import numpy as np
import jax
import jax.numpy as jnp
from jax import lax
from jax.experimental import pallas as pl
from jax.experimental.pallas import tpu as pltpu

F32 = jnp.float32
BF16 = jnp.bfloat16

EPS = 1e-6
CHUNK = 64
N_LEFT_CHUNKS = 8
REL_MAX = 4 * CHUNK
ATT_HEADS = 8
HEAD_DIM = 64
ATT_WIDTH = ATT_HEADS * HEAD_DIM
CONV_TAPS = 31
GMLP_CHUNK = 128
GMLP_GROUPS = 8
FFN_TAPS = 3
NEG_INF = -1e30
LOG2E = 1.4426950408889634

SUBLANES = 8
LANES = 128
VMEM_LIMIT_CAP_BYTES = 60000 * 1024

ROW_TILE = 512
EVEN_TILE = 1024
CONV_TILE = 1024
GMLP_TILE = 1024
ATT_TILE = 256
ATT_KEYS = 3 * ATT_TILE
ATT_SUBS = 2
FF_CHUNK = 256
CONV_HALO = 32
CONV_SUB = 32


def _dot(a, b):
    return jnp.dot(a, b, preferred_element_type=F32)


def _dot_nt(a, b):
    return lax.dot_general(a, b, (((1,), (1,)), ((), ())), preferred_element_type=F32)


def _rms(x, g):
    return x * lax.rsqrt(jnp.mean(x * x, axis=-1, keepdims=True) + EPS) * g


def _gelu_tanh(x):
    a = float(-2.0 * np.sqrt(2.0 / np.pi) * LOG2E)
    b = a * 0.044715
    return x * (1.0 / (1.0 + jnp.exp2(x * (b * (x * x) + a))))


def _nbytes(shape, dtype):
    return int(np.prod(shape)) * jnp.dtype(dtype).itemsize


def _vmem_limit(pipelined, resident, scratch):
    need = 2 * sum(pipelined) + sum(resident) + sum(scratch)
    need += 4 * max(pipelined)
    return int(min(VMEM_LIMIT_CAP_BYTES, max(need, 16 * 1024 * 1024)))


def _resident(shape):
    nd = len(shape)
    return pl.BlockSpec(shape, lambda *_: (0,) * nd, pipeline_mode=pl.Buffered(1))


def _even_in_kernel(x_ref, g_ref, w_ref, qg_ref, kg_ref,
                    a_ref, q_ref, k_ref, v_ref, h_ref):
    cw = a_ref.shape[-1]
    x = x_ref[...]
    h_ref[...] = _rms(x, g_ref[...]).astype(BF16)

    a_lin = _dot(h_ref[...], w_ref[:, 0:cw])
    a_gate = _dot(h_ref[...], w_ref[:, cw:2 * cw])
    a_ref[...] = a_lin * jax.nn.sigmoid(a_gate)

    def head_norm(z, gain):
        s = z * z
        low = lax.broadcasted_iota(jnp.int32, (1, 2 * HEAD_DIM), 1) < HEAD_DIM
        parts = []
        for c in range(0, s.shape[1], 2 * HEAD_DIM):
            sg = s[:, c:c + 2 * HEAD_DIM]
            s_low = jnp.sum(jnp.where(low, sg, 0.0), axis=-1, keepdims=True)
            s_high = jnp.sum(jnp.where(low, 0.0, sg), axis=-1, keepdims=True)
            parts.append(jnp.where(low, s_low, s_high))
        ss = jnp.concatenate(parts, axis=1)
        return z * lax.rsqrt(ss * (1.0 / HEAD_DIM) + EPS) * gain

    base = 2 * cw
    q = _dot(h_ref[...], w_ref[:, base:base + ATT_WIDTH])
    q_ref[...] = (head_norm(q, qg_ref[...]) * (HEAD_DIM ** -0.5 * LOG2E)).astype(BF16)
    k = _dot(h_ref[...], w_ref[:, base + ATT_WIDTH:base + 2 * ATT_WIDTH])
    k_ref[...] = head_norm(k, kg_ref[...]).astype(BF16)
    v = _dot(h_ref[...], w_ref[:, base + 2 * ATT_WIDTH:base + 3 * ATT_WIDTH])
    v_ref[...] = v.astype(BF16)


def _even_in(x2, g, w_bf, qg, kg):
    t, d = x2.shape
    n = w_bf.shape[1]
    cw = (n - 3 * ATT_WIDTH) // 2
    tm = EVEN_TILE
    row = lambda width: pl.BlockSpec((tm, width), lambda i: (i, 0))
    limit = _vmem_limit(
        [_nbytes((tm, d), F32), _nbytes((tm, cw), F32)] + 3 * [_nbytes((tm, ATT_WIDTH), BF16)],
        [_nbytes(w_bf.shape, BF16)],
        [_nbytes((tm, d), BF16), 4 * _nbytes((tm, 2 * cw), F32)])
    return pl.pallas_call(
        _even_in_kernel,
        grid=(t // tm,),
        in_specs=[row(d), _resident((1, d)), _resident(w_bf.shape),
                  _resident((1, ATT_WIDTH)), _resident((1, ATT_WIDTH))],
        out_specs=[row(cw), row(ATT_WIDTH), row(ATT_WIDTH), row(ATT_WIDTH)],
        out_shape=[jax.ShapeDtypeStruct((t, cw), F32)] +
                  3 * [jax.ShapeDtypeStruct((t, ATT_WIDTH), BF16)],
        scratch_shapes=[pltpu.VMEM((tm, d), BF16)],
        compiler_params=pltpu.CompilerParams(
            dimension_semantics=("parallel",), vmem_limit_bytes=limit),
        name="even_in",
    )(x2, g, w_bf, qg, kg)


def _conv_module_kernel(prev_ref, cur_ref, w_ref, b_ref, lg_ref, lb_ref, o_ref, buf_ref, sh_ref):
    ts = cur_ref.shape[0]
    halo = prev_ref.shape[0]
    taps = w_ref.shape[0]
    span = sh_ref.shape[1]
    first = pl.program_id(1) == 0
    buf_ref[0:halo, :] = jnp.where(first, 0.0, prev_ref[...])
    buf_ref[halo:halo + ts, :] = cur_ref[...]
    for s in range(1, SUBLANES):
        for r in range(0, span, CONV_SUB):
            n = min(CONV_SUB, span - r)
            sh_ref[s - 1, r:r + n, :] = buf_ref[r + s:r + s + n, :]
    off = halo - (taps - 1)
    taps_w = [jnp.concatenate([w_ref[k]] * (CONV_SUB // SUBLANES), axis=0) for k in range(taps)]
    for r in range(0, ts, CONV_SUB):
        acc = jnp.zeros((CONV_SUB, cur_ref.shape[1]), F32) + b_ref[...]
        for k in range(taps):
            s = (off + k) % SUBLANES
            lo = r + off + k - s
            rows = buf_ref[lo:lo + CONV_SUB, :] if s == 0 else sh_ref[s - 1, lo:lo + CONV_SUB, :]
            acc = acc + taps_w[k] * rows
        mu = jnp.mean(acc, axis=-1, keepdims=True)
        cen = acc - mu
        var = jnp.mean(cen * cen, axis=-1, keepdims=True)
        y = cen * lax.rsqrt(var + EPS) * lg_ref[...] + lb_ref[...]
        o_ref[r:r + CONV_SUB, :] = (y * jax.nn.sigmoid(y)).astype(o_ref.dtype)


def _conv_module(a2, batch, conv_w, conv_b, ln_g, ln_b):
    t, c = a2.shape
    ts = CONV_TILE
    nt = t // batch // ts
    per = ts // CONV_HALO
    shifted = (SUBLANES - 1, ts + CONV_HALO - SUBLANES, c)
    limit = _vmem_limit([_nbytes((ts, c), F32), _nbytes((CONV_HALO, c), F32),
                         _nbytes((ts, c), BF16)],
                        [_nbytes(conv_w.shape, F32)],
                        [_nbytes((ts + CONV_HALO, c), F32), _nbytes(shifted, F32)])
    return pl.pallas_call(
        _conv_module_kernel,
        grid=(batch, nt),
        in_specs=[
            pl.BlockSpec((CONV_HALO, c), lambda b, i: (jnp.maximum((b * nt + i) * per - 1, 0), 0)),
            pl.BlockSpec((ts, c), lambda b, i: (b * nt + i, 0)),
            _resident(conv_w.shape), _resident((1, c)), _resident((1, c)), _resident((1, c))],
        out_specs=pl.BlockSpec((ts, c), lambda b, i: (b * nt + i, 0)),
        out_shape=jax.ShapeDtypeStruct((t, c), BF16),
        scratch_shapes=[pltpu.VMEM((ts + CONV_HALO, c), F32), pltpu.VMEM(shifted, F32)],
        compiler_params=pltpu.CompilerParams(
            dimension_semantics=("parallel", "parallel"), vmem_limit_bytes=limit),
        name="conv_module",
    )(a2, a2, conv_w, conv_b, ln_g, ln_b)


def _attn_kernel(*refs):
    n_kv = ATT_SUBS + 2
    q_ref, k_refs, v_refs = refs[0], refs[1:1 + n_kv], refs[1 + n_kv:1 + 2 * n_kv]
    bias_refs, o_ref = refs[1 + 2 * n_kv:-1], refs[-1]
    low = lax.broadcasted_iota(jnp.int32, (1, 2 * HEAD_DIM), 1) < HEAD_DIM
    keep = [jnp.where(low, 1.0, 0.0).astype(BF16), jnp.where(low, 0.0, 1.0).astype(BF16)]
    tq = ATT_TILE
    for j in range(ATT_SUBS):
        rows = slice(j * tq, (j + 1) * tq)
        for p in range(ATT_HEADS // 2):
            cols = slice(2 * HEAD_DIM * p, 2 * HEAD_DIM * (p + 1))
            qp = q_ref[rows, cols]
            kp = jnp.concatenate([r[:, cols] for r in k_refs[j:j + 3]], axis=0)
            vp = jnp.concatenate([r[:, cols] for r in v_refs[j:j + 3]], axis=0)
            q2 = jnp.concatenate([qp * keep[0], qp * keep[1]], axis=0)
            bias = bias_refs[j][0, 2 * p:2 * p + 2].reshape(2 * tq, kp.shape[0])
            s = _dot_nt(q2, kp) + bias
            m = jnp.max(s, axis=-1, keepdims=True)
            pe = jnp.exp2(s - m)
            den = jnp.sum(pe, axis=-1, keepdims=True)
            o2 = _dot(pe.astype(BF16), vp) * (1.0 / den)
            o_ref[rows, cols] = jnp.where(low, o2[0:tq], o2[tq:2 * tq]).astype(o_ref.dtype)


def _attn_bias_table(rel_bias):
    heads = rel_bias.shape[0]
    band_keys = (N_LEFT_CHUNKS + 1) * CHUNK
    period = CHUNK + band_keys
    delta = np.arange(period)
    delta = np.where(delta < band_keys, delta, delta - period)
    idx = np.clip(N_LEFT_CHUNKS * CHUNK - delta, -(CHUNK - 1), REL_MAX) + (CHUNK - 1)
    one_period = rel_bias[:, idx].astype(F32) * LOG2E
    flat = jnp.tile(one_period, (1, CHUNK))[:, :CHUNK * (period - 1)]
    block = flat.reshape(heads, CHUNK, period - 1)[:, :, :band_keys]
    rows = [jnp.pad(block, ((0, 0), (0, 0), (j * CHUNK, ATT_KEYS - band_keys - j * CHUNK)),
                    constant_values=NEG_INF) for j in range(ATT_TILE // CHUNK)]
    table = jnp.concatenate(rows, axis=1)
    kk = np.arange(ATT_KEYS)[None, :]
    present = np.stack([kk >= (2 - v) * ATT_TILE for v in range(3)])
    return jnp.where(present[:, None], table[None], NEG_INF)


def _attention(q2, k2, v2, batch, bias_tab):
    t, w = q2.shape
    tq = ATT_TILE
    step = ATT_SUBS * tq
    nt = t // batch // step
    n_kv = ATT_SUBS + 2
    q_blk = pl.BlockSpec((step, w), lambda b, i: (b * nt + i, 0))
    kv_blk = lambda m: pl.BlockSpec(
        (tq, w), lambda b, i: (b * nt * ATT_SUBS + jnp.maximum(ATT_SUBS * i - 2 + m, 0), 0))
    bias_blk = lambda j: pl.BlockSpec(
        (1,) + bias_tab.shape[1:], lambda b, i: (jnp.minimum(ATT_SUBS * i + j, 2), 0, 0, 0),
        pipeline_mode=pl.Buffered(1))
    limit = _vmem_limit([2 * _nbytes((step, w), BF16)] + 2 * n_kv * [_nbytes((tq, w), BF16)],
                        ATT_SUBS * [_nbytes(bias_tab.shape[1:], F32)],
                        [8 * _nbytes((2 * tq, ATT_KEYS), F32)])
    return pl.pallas_call(
        _attn_kernel,
        grid=(batch, nt),
        in_specs=[q_blk] + 2 * [kv_blk(m) for m in range(n_kv)] +
                 [bias_blk(j) for j in range(ATT_SUBS)],
        out_specs=q_blk,
        out_shape=jax.ShapeDtypeStruct((t, w), BF16),
        compiler_params=pltpu.CompilerParams(
            dimension_semantics=("parallel", "parallel"), vmem_limit_bytes=limit),
        name="chunk_attention",
    )(q2, *(n_kv * [k2]), *(n_kv * [v2]), *(ATT_SUBS * [bias_tab]))


def _gmlp_in_kernel(x_ref, g_ref, w_ref, lg_ref, lb_ref, wsp_ref, bsp_ref, y_ref,
                    h_ref, v_ref, vn_ref):
    tm = x_ref.shape[0]
    width = v_ref.shape[1]
    gd = width // GMLP_GROUPS
    h_ref[...] = _rms(x_ref[...], g_ref[...]).astype(BF16)

    total = jnp.zeros((tm, 1), F32)
    for g in range(GMLP_GROUPS):
        zv = _gelu_tanh(_dot(h_ref[...], w_ref[:, width + g * gd:width + (g + 1) * gd]))
        v_ref[:, g * gd:(g + 1) * gd] = zv
        total = total + jnp.sum(zv, axis=-1, keepdims=True)
    mu = total * (1.0 / width)
    sq = jnp.zeros((tm, 1), F32)
    for g in range(GMLP_GROUPS):
        cen = v_ref[:, g * gd:(g + 1) * gd] - mu
        sq = sq + jnp.sum(cen * cen, axis=-1, keepdims=True)
    rstd = lax.rsqrt(sq * (1.0 / width) + EPS)
    for g in range(GMLP_GROUPS):
        sl = slice(g * gd, (g + 1) * gd)
        vn_ref[:, sl] = ((v_ref[:, sl] - mu) * rstd * lg_ref[:, sl] + lb_ref[:, sl]).astype(BF16)

    row = lax.broadcasted_iota(jnp.int32, (GMLP_CHUNK, GMLP_CHUNK), 0)
    col = lax.broadcasted_iota(jnp.int32, (GMLP_CHUNK, GMLP_CHUNK), 1)
    causal = row >= col
    for g in range(GMLP_GROUPS):
        sl = slice(g * gd, (g + 1) * gd)
        u = _gelu_tanh(_dot(h_ref[...], w_ref[:, sl]))
        wm = jnp.where(causal, wsp_ref[g], 0.0).astype(BF16)
        bias = jnp.concatenate([bsp_ref[g]] * (gd // LANES), axis=-1)
        for r in range(0, tm, GMLP_CHUNK):
            sv = _dot(wm, vn_ref[r:r + GMLP_CHUNK, sl]) + bias
            y_ref[r:r + GMLP_CHUNK, sl] = (u[r:r + GMLP_CHUNK] * sv).astype(y_ref.dtype)


def _gmlp_in(x2, g, w_bf, ln_g, ln_b, w_sp, b_sp_tile):
    t, d = x2.shape
    width = w_bf.shape[1] // 2
    tm = GMLP_TILE
    limit = _vmem_limit(
        [_nbytes((tm, d), F32), _nbytes((tm, width), BF16)],
        [_nbytes(w_bf.shape, BF16), _nbytes(w_sp.shape, F32), _nbytes(b_sp_tile.shape, F32)],
        [_nbytes((tm, d), BF16), _nbytes((tm, width), F32), _nbytes((tm, width), BF16),
         8 * _nbytes((tm, width // GMLP_GROUPS), F32)])
    return pl.pallas_call(
        _gmlp_in_kernel,
        grid=(t // tm,),
        in_specs=[pl.BlockSpec((tm, d), lambda i: (i, 0)), _resident((1, d)),
                  _resident(w_bf.shape), _resident((1, width)), _resident((1, width)),
                  _resident(w_sp.shape), _resident(b_sp_tile.shape)],
        out_specs=pl.BlockSpec((tm, width), lambda i: (i, 0)),
        out_shape=jax.ShapeDtypeStruct((t, width), BF16),
        scratch_shapes=[pltpu.VMEM((tm, d), BF16), pltpu.VMEM((tm, width), F32),
                        pltpu.VMEM((tm, width), BF16)],
        compiler_params=pltpu.CompilerParams(
            dimension_semantics=("parallel",), vmem_limit_bytes=limit),
        name="gmlp_in",
    )(x2, g, w_bf, ln_g, ln_b, w_sp, b_sp_tile)


def _mix_ffn_kernel(n_mix, *refs):
    x_ref = refs[0]
    y_refs = refs[1:1 + n_mix]
    wm_refs = refs[1 + n_mix:1 + 2 * n_mix]
    (g_ref, wg_ref, wu_ref, dww_ref, dwb_ref, wd_ref,
     o_ref, h_ref, act_ref, carry_ref) = refs[1 + 2 * n_mix:]
    tm = x_ref.shape[0]
    d_ff = wg_ref.shape[1]
    fc = FF_CHUNK

    @pl.when(pl.program_id(1) == 0)
    def _():
        carry_ref[...] = jnp.zeros_like(carry_ref)

    x1 = x_ref[...]
    for y_ref, wm_ref in zip(y_refs, wm_refs):
        x1 = x1 + _dot(y_ref[...], wm_ref[...])
    o_ref[...] = x1
    h_ref[...] = _rms(x1, g_ref[...]).astype(BF16)

    for c in range(0, d_ff, fc):
        cols = slice(c, c + fc)
        gate = _dot(h_ref[...], wg_ref[:, cols])
        up = _dot(h_ref[...], wu_ref[:, cols])
        ext = jnp.concatenate([carry_ref[:, cols], gate], axis=0)
        carry_ref[:, cols] = gate[tm - SUBLANES:tm, :]
        conv = dwb_ref[:, cols] + dww_ref[FFN_TAPS - 1:FFN_TAPS, cols] * gate
        for k in range(FFN_TAPS - 2, -1, -1):
            ext = pltpu.roll(ext, 1, axis=0)
            conv = conv + dww_ref[k:k + 1, cols] * ext[SUBLANES:SUBLANES + tm]
        act_ref[:, cols] = (conv * jax.nn.sigmoid(conv) * up).astype(BF16)

    o_ref[...] = o_ref[...] + _dot(act_ref[...], wd_ref[...])


def _mix_ffn(x2, batch, mix_pairs, g, wg, wu, dww, dwb, wd):
    t, d = x2.shape
    tm = ROW_TILE
    nt = t // batch // tm
    d_ff = wg.shape[1]
    fc = FF_CHUNK
    n_mix = len(mix_pairs)
    row = lambda width: pl.BlockSpec((tm, width), lambda b, i: (b * nt + i, 0))
    ys = [y for y, _ in mix_pairs]
    wms = [w for _, w in mix_pairs]
    limit = _vmem_limit(
        2 * [_nbytes((tm, d), F32)] + [_nbytes((tm, y.shape[1]), BF16) for y in ys],
        [_nbytes(w.shape, BF16) for w in wms] + 3 * [_nbytes(wg.shape, BF16)] +
        [_nbytes((16, d_ff), F32)],
        [_nbytes((tm, d), BF16), _nbytes((tm, d_ff), BF16),
         _nbytes((tm + SUBLANES, fc), F32), _nbytes((SUBLANES, d_ff), F32),
         4 * _nbytes((tm, fc), F32)])
    return pl.pallas_call(
        functools.partial(_mix_ffn_kernel, n_mix),
        grid=(batch, nt),
        in_specs=[row(d)] + [row(y.shape[1]) for y in ys] + [_resident(w.shape) for w in wms] +
                 [_resident((1, d)), _resident(wg.shape), _resident(wu.shape),
                  _resident(dww.shape), _resident(dwb.shape), _resident(wd.shape)],
        out_specs=row(d),
        out_shape=jax.ShapeDtypeStruct((t, d), F32),
        scratch_shapes=[pltpu.VMEM((tm, d), BF16), pltpu.VMEM((tm, d_ff), BF16),
                        pltpu.VMEM((SUBLANES, d_ff), F32)],
        compiler_params=pltpu.CompilerParams(
            dimension_semantics=("arbitrary", "arbitrary"), vmem_limit_bytes=limit),
        name="mix_ffn",
    )(x2, *ys, *wms, g, wg, wu, dww, dwb, wd)


def kernel(x, mix_norm_g, w_in_e, conv_w, conv_b, conv_ln_g, conv_ln_b, qn_g, kn_g, rel_bias,
           w_out_e, w_in_o, gate_ln_g, gate_ln_b, w_sp, b_sp, w_out_o, ffn_norm_g, w_gate, w_up,
           ffn_dw_w, ffn_dw_b, w_down):
    batch, seq, d = x.shape
    depth = mix_norm_g.shape[0]
    d_ff = w_gate.shape[-1]
    assert seq % ROW_TILE == 0 and seq % (ATT_SUBS * ATT_TILE) == 0 and d_ff % FF_CHUNK == 0
    assert seq % GMLP_TILE == 0 and GMLP_TILE % GMLP_CHUNK == 0
    assert seq % EVEN_TILE == 0 and seq % CONV_TILE == 0
    assert CONV_TILE % CONV_HALO == 0 and CONV_TILE % CONV_SUB == 0 and ATT_TILE % CHUNK == 0
    assert ATT_KEYS >= ATT_TILE + N_LEFT_CHUNKS * CHUNK and CONV_HALO >= CONV_TAPS - 1

    x2 = x.reshape(batch * seq, d)
    row = lambda vec: vec.reshape(1, -1).astype(F32)

    for layer in range(depth):
        i = layer // 2
        if layer % 2 == 0:
            a, q, k, v = _even_in(
                x2, row(mix_norm_g[layer]), w_in_e[i].astype(BF16),
                row(jnp.tile(qn_g[i], ATT_HEADS)), row(jnp.tile(kn_g[i], ATT_HEADS)))
            taps_w = jnp.broadcast_to(conv_w[i].astype(F32)[:, None, :],
                                      (conv_w.shape[1], SUBLANES, conv_w.shape[2]))
            a = _conv_module(a, batch, taps_w, row(conv_b[i]),
                             row(conv_ln_g[i]), row(conv_ln_b[i]))
            o = _attention(q, k, v, batch, _attn_bias_table(rel_bias[i]))
            cw = a.shape[1]
            w_out = w_out_e[i].astype(BF16)
            mix_pairs = [(a, w_out[:cw]), (o, w_out[cw:])]
        else:
            b_tile = jnp.broadcast_to(b_sp[i].astype(F32)[:, :, None],
                                      (GMLP_GROUPS, GMLP_CHUNK, LANES))
            y = _gmlp_in(x2, row(mix_norm_g[layer]), w_in_o[i].astype(BF16),
                         row(gate_ln_g[i]), row(gate_ln_b[i]), w_sp[i].astype(F32), b_tile)
            mix_pairs = [(y, w_out_o[i].astype(BF16))]
        x2 = _mix_ffn(
            x2, batch, mix_pairs, row(ffn_norm_g[layer]),
            w_gate[layer].astype(BF16), w_up[layer].astype(BF16),
            ffn_dw_w[layer].astype(F32), row(ffn_dw_b[layer]), w_down[layer].astype(BF16))
    return x2.reshape(batch, seq, d)
```

```python
import functools

import numpy as np
import jax
import jax.numpy as jnp
from jax import lax
from jax.experimental import pallas as pl
from jax.experimental.pallas import tpu as pltpu

F32 = jnp.float32
BF16 = jnp.bfloat16

EPS = 1e-6
CHUNK = 64
N_LEFT_CHUNKS = 8
REL_MAX = 4 * CHUNK
ATT_HEADS = 8
HEAD_DIM = 64
ATT_WIDTH = ATT_HEADS * HEAD_DIM
CONV_TAPS = 31
GMLP_CHUNK = 128
GMLP_GROUPS = 8
FFN_TAPS = 3
NEG_INF = -1e30
LOG2E = 1.4426950408889634

SUBLANES = 8
LANES = 128
VMEM_LIMIT_CAP_BYTES = 60000 * 1024

ROW_TILE = 512
FFN_VMEM_HEADROOM = 0.10
EVEN_TILE = 1024
CONV_TILE = 1024
GMLP_TILE = 1024
ATT_TILE = 256
ATT_KEYS = 3 * ATT_TILE
ATT_SUBS = 4
FF_CHUNK = 256
CONV_HALO = 32
CONV_SUB = 32


def _dot(a, b):
    return jnp.dot(a, b, preferred_element_type=F32)


def _dot_nt(a, b):
    return lax.dot_general(a, b, (((1,), (1,)), ((), ())), preferred_element_type=F32)


def _rms(x, g):
    return x * lax.rsqrt(jnp.mean(x * x, axis=-1, keepdims=True) + EPS) * g


def _gelu_tanh(x):
    a = float(-2.0 * np.sqrt(2.0 / np.pi) * LOG2E)
    b = a * 0.044715
    return x * (1.0 / (1.0 + jnp.exp2(x * (b * (x * x) + a))))


def _nbytes(shape, dtype):
    return int(np.prod(shape)) * jnp.dtype(dtype).itemsize


def _vmem_limit(pipelined, resident, scratch):
    need = 2 * sum(pipelined) + sum(resident) + sum(scratch)
    need += 4 * max(pipelined)
    return int(min(VMEM_LIMIT_CAP_BYTES, max(need, 16 * 1024 * 1024)))


def _resident(shape):
    nd = len(shape)
    return pl.BlockSpec(shape, lambda *_: (0,) * nd, pipeline_mode=pl.Buffered(1))


def _even_in_kernel(x_ref, g_ref, w_ref, qg_ref, kg_ref,
                    a_ref, q_ref, k_ref, v_ref, h_ref):
    cw = a_ref.shape[-1]
    x = x_ref[...]
    h_ref[...] = _rms(x, g_ref[...]).astype(BF16)

    a_lin = _dot(h_ref[...], w_ref[:, 0:cw])
    a_gate = _dot(h_ref[...], w_ref[:, cw:2 * cw])
    a_ref[...] = a_lin * jax.nn.sigmoid(a_gate)

    def head_norm(z, gain):
        s = z * z
        low = lax.broadcasted_iota(jnp.int32, (1, 2 * HEAD_DIM), 1) < HEAD_DIM
        parts = []
        for c in range(0, s.shape[1], 2 * HEAD_DIM):
            sg = s[:, c:c + 2 * HEAD_DIM]
            s_low = jnp.sum(jnp.where(low, sg, 0.0), axis=-1, keepdims=True)
            s_high = jnp.sum(jnp.where(low, 0.0, sg), axis=-1, keepdims=True)
            parts.append(jnp.where(low, s_low, s_high))
        ss = jnp.concatenate(parts, axis=1)
        return z * lax.rsqrt(ss * (1.0 / HEAD_DIM) + EPS) * gain

    base = 2 * cw
    q = _dot(h_ref[...], w_ref[:, base:base + ATT_WIDTH])
    q_ref[...] = (head_norm(q, qg_ref[...]) * (HEAD_DIM ** -0.5 * LOG2E)).astype(BF16)
    k = _dot(h_ref[...], w_ref[:, base + ATT_WIDTH:base + 2 * ATT_WIDTH])
    k_ref[...] = head_norm(k, kg_ref[...]).astype(BF16)
    v = _dot(h_ref[...], w_ref[:, base + 2 * ATT_WIDTH:base + 3 * ATT_WIDTH])
    v_ref[...] = v.astype(BF16)


def _even_in(x2, g, w_bf, qg, kg):
    t, d = x2.shape
    n = w_bf.shape[1]
    cw = (n - 3 * ATT_WIDTH) // 2
    tm = EVEN_TILE
    row = lambda width: pl.BlockSpec((tm, width), lambda i: (i, 0))
    limit = _vmem_limit(
        [_nbytes((tm, d), F32), _nbytes((tm, cw), F32)] + 3 * [_nbytes((tm, ATT_WIDTH), BF16)],
        [_nbytes(w_bf.shape, BF16)],
        [_nbytes((tm, d), BF16), 4 * _nbytes((tm, 2 * cw), F32)])
    return pl.pallas_call(
        _even_in_kernel,
        grid=(t // tm,),
        in_specs=[row(d), _resident((1, d)), _resident(w_bf.shape),
                  _resident((1, ATT_WIDTH)), _resident((1, ATT_WIDTH))],
        out_specs=[row(cw), row(ATT_WIDTH), row(ATT_WIDTH), row(ATT_WIDTH)],
        out_shape=[jax.ShapeDtypeStruct((t, cw), F32)] +
                  3 * [jax.ShapeDtypeStruct((t, ATT_WIDTH), BF16)],
        scratch_shapes=[pltpu.VMEM((tm, d), BF16)],
        compiler_params=pltpu.CompilerParams(
            dimension_semantics=("parallel",), vmem_limit_bytes=limit),
        name="even_in",
    )(x2, g, w_bf, qg, kg)


def _conv_module_kernel(prev_ref, cur_ref, w_ref, b_ref, lg_ref, lb_ref, o_ref, buf_ref, sh_ref):
    ts = cur_ref.shape[0]
    halo = prev_ref.shape[0]
    taps = w_ref.shape[0]
    span = sh_ref.shape[1]
    first = pl.program_id(1) == 0
    buf_ref[0:halo, :] = jnp.where(first, 0.0, prev_ref[...])
    buf_ref[halo:halo + ts, :] = cur_ref[...]
    for s in range(1, SUBLANES):
        for r in range(0, span, CONV_SUB):
            n = min(CONV_SUB, span - r)
            sh_ref[s - 1, r:r + n, :] = buf_ref[r + s:r + s + n, :]
    off = halo - (taps - 1)
    taps_w = [jnp.concatenate([w_ref[k]] * (CONV_SUB // SUBLANES), axis=0) for k in range(taps)]
    for r in range(0, ts, CONV_SUB):
        acc = jnp.zeros((CONV_SUB, cur_ref.shape[1]), F32) + b_ref[...]
        for k in range(taps):
            s = (off + k) % SUBLANES
            lo = r + off + k - s
            rows = buf_ref[lo:lo + CONV_SUB, :] if s == 0 else sh_ref[s - 1, lo:lo + CONV_SUB, :]
            acc = acc + taps_w[k] * rows
        mu = jnp.mean(acc, axis=-1, keepdims=True)
        cen = acc - mu
        var = jnp.mean(cen * cen, axis=-1, keepdims=True)
        y = cen * lax.rsqrt(var + EPS) * lg_ref[...] + lb_ref[...]
        o_ref[r:r + CONV_SUB, :] = (y * jax.nn.sigmoid(y)).astype(o_ref.dtype)


def _conv_module(a2, batch, conv_w, conv_b, ln_g, ln_b):
    t, c = a2.shape
    ts = CONV_TILE
    nt = t // batch // ts
    per = ts // CONV_HALO
    shifted = (SUBLANES - 1, ts + CONV_HALO - SUBLANES, c)
    limit = _vmem_limit([_nbytes((ts, c), F32), _nbytes((CONV_HALO, c), F32),
                         _nbytes((ts, c), BF16)],
                        [_nbytes(conv_w.shape, F32)],
                        [_nbytes((ts + CONV_HALO, c), F32), _nbytes(shifted, F32)])
    return pl.pallas_call(
        _conv_module_kernel,
        grid=(batch, nt),
        in_specs=[
            pl.BlockSpec((CONV_HALO, c), lambda b, i: (jnp.maximum((b * nt + i) * per - 1, 0), 0)),
            pl.BlockSpec((ts, c), lambda b, i: (b * nt + i, 0)),
            _resident(conv_w.shape), _resident((1, c)), _resident((1, c)), _resident((1, c))],
        out_specs=pl.BlockSpec((ts, c), lambda b, i: (b * nt + i, 0)),
        out_shape=jax.ShapeDtypeStruct((t, c), BF16),
        scratch_shapes=[pltpu.VMEM((ts + CONV_HALO, c), F32), pltpu.VMEM(shifted, F32)],
        compiler_params=pltpu.CompilerParams(
            dimension_semantics=("parallel", "parallel"), vmem_limit_bytes=limit),
        name="conv_module",
    )(a2, a2, conv_w, conv_b, ln_g, ln_b)


def _attn_kernel(*refs):
    n_kv = ATT_SUBS + 2
    q_ref, k_refs, v_refs = refs[0], refs[1:1 + n_kv], refs[1 + n_kv:1 + 2 * n_kv]
    bias_refs, o_ref = refs[1 + 2 * n_kv:-1], refs[-1]
    low = lax.broadcasted_iota(jnp.int32, (1, 2 * HEAD_DIM), 1) < HEAD_DIM
    keep = [jnp.where(low, 1.0, 0.0).astype(BF16), jnp.where(low, 0.0, 1.0).astype(BF16)]
    tq = ATT_TILE
    for j in range(ATT_SUBS):
        rows = slice(j * tq, (j + 1) * tq)
        for p in range(ATT_HEADS // 2):
            cols = slice(2 * HEAD_DIM * p, 2 * HEAD_DIM * (p + 1))
            qp = q_ref[rows, cols]
            kp = jnp.concatenate([r[:, cols] for r in k_refs[j:j + 3]], axis=0)
            vp = jnp.concatenate([r[:, cols] for r in v_refs[j:j + 3]], axis=0)
            q2 = jnp.concatenate([qp * keep[0], qp * keep[1]], axis=0)
            bias_ref = bias_refs[min(j, len(bias_refs) - 1)]
            bias = bias_ref[0, 2 * p:2 * p + 2].reshape(2 * tq, kp.shape[0])
            s = _dot_nt(q2, kp) + bias
            m = jnp.max(s, axis=-1, keepdims=True)
            pe = jnp.exp2(s - m)
            den = jnp.sum(pe, axis=-1, keepdims=True)
            o2 = _dot(pe.astype(BF16), vp) * (1.0 / den)
            o_ref[rows, cols] = jnp.where(low, o2[0:tq], o2[tq:2 * tq]).astype(o_ref.dtype)


def _attn_bias_table(rel_bias):
    heads = rel_bias.shape[0]
    band_keys = (N_LEFT_CHUNKS + 1) * CHUNK
    period = CHUNK + band_keys
    delta = np.arange(period)
    delta = np.where(delta < band_keys, delta, delta - period)
    idx = np.clip(N_LEFT_CHUNKS * CHUNK - delta, -(CHUNK - 1), REL_MAX) + (CHUNK - 1)
    one_period = rel_bias[:, idx].astype(F32) * LOG2E
    flat = jnp.tile(one_period, (1, CHUNK))[:, :CHUNK * (period - 1)]
    block = flat.reshape(heads, CHUNK, period - 1)[:, :, :band_keys]
    rows = [jnp.pad(block, ((0, 0), (0, 0), (j * CHUNK, ATT_KEYS - band_keys - j * CHUNK)),
                    constant_values=NEG_INF) for j in range(ATT_TILE // CHUNK)]
    table = jnp.concatenate(rows, axis=1)
    kk = np.arange(ATT_KEYS)[None, :]
    present = np.stack([kk >= (2 - v) * ATT_TILE for v in range(3)])
    return jnp.where(present[:, None], table[None], NEG_INF)


def _attention(q2, k2, v2, batch, bias_tab):
    t, w = q2.shape
    tq = ATT_TILE
    step = ATT_SUBS * tq
    nt = t // batch // step
    n_kv = ATT_SUBS + 2
    q_blk = pl.BlockSpec((step, w), lambda b, i: (b * nt + i, 0))
    kv_blk = lambda m: pl.BlockSpec(
        (tq, w), lambda b, i: (b * nt * ATT_SUBS + jnp.maximum(ATT_SUBS * i - 2 + m, 0), 0))
    n_bias = min(ATT_SUBS, 3)
    bias_blk = lambda j: pl.BlockSpec(
        (1,) + bias_tab.shape[1:], lambda b, i: (jnp.minimum(ATT_SUBS * i + j, 2), 0, 0, 0),
        pipeline_mode=pl.Buffered(1))
    limit = _vmem_limit([2 * _nbytes((step, w), BF16)] + 2 * n_kv * [_nbytes((tq, w), BF16)],
                        n_bias * [_nbytes(bias_tab.shape[1:], F32)],
                        [8 * _nbytes((2 * tq, ATT_KEYS), F32)])
    return pl.pallas_call(
        _attn_kernel,
        grid=(batch, nt),
        in_specs=[q_blk] + 2 * [kv_blk(m) for m in range(n_kv)] +
                 [bias_blk(j) for j in range(n_bias)],
        out_specs=q_blk,
        out_shape=jax.ShapeDtypeStruct((t, w), BF16),
        compiler_params=pltpu.CompilerParams(
            dimension_semantics=("parallel", "parallel"), vmem_limit_bytes=limit),
        name="chunk_attention",
    )(q2, *(n_kv * [k2]), *(n_kv * [v2]), *(n_bias * [bias_tab]))


def _gmlp_in_kernel(x_ref, g_ref, w_ref, lg_ref, lb_ref, wsp_ref, bsp_ref, y_ref,
                    h_ref, v_ref, vn_ref):
    tm = x_ref.shape[0]
    width = v_ref.shape[1]
    gd = width // GMLP_GROUPS
    h_ref[...] = _rms(x_ref[...], g_ref[...]).astype(BF16)

    total = jnp.zeros((tm, 1), F32)
    for g in range(GMLP_GROUPS):
        zv = _gelu_tanh(_dot(h_ref[...], w_ref[:, width + g * gd:width + (g + 1) * gd]))
        v_ref[:, g * gd:(g + 1) * gd] = zv
        total = total + jnp.sum(zv, axis=-1, keepdims=True)
    mu = total * (1.0 / width)
    sq = jnp.zeros((tm, 1), F32)
    for g in range(GMLP_GROUPS):
        cen = v_ref[:, g * gd:(g + 1) * gd] - mu
        sq = sq + jnp.sum(cen * cen, axis=-1, keepdims=True)
    rstd = lax.rsqrt(sq * (1.0 / width) + EPS)
    for g in range(GMLP_GROUPS):
        sl = slice(g * gd, (g + 1) * gd)
        vn_ref[:, sl] = ((v_ref[:, sl] - mu) * rstd * lg_ref[:, sl] + lb_ref[:, sl]).astype(BF16)

    row = lax.broadcasted_iota(jnp.int32, (GMLP_CHUNK, GMLP_CHUNK), 0)
    col = lax.broadcasted_iota(jnp.int32, (GMLP_CHUNK, GMLP_CHUNK), 1)
    causal = row >= col
    for g in range(GMLP_GROUPS):
        sl = slice(g * gd, (g + 1) * gd)
        u = _gelu_tanh(_dot(h_ref[...], w_ref[:, sl]))
        wm = jnp.where(causal, wsp_ref[g], 0.0).astype(BF16)
        bias = jnp.concatenate([bsp_ref[g]] * (gd // LANES), axis=-1)
        for r in range(0, tm, GMLP_CHUNK):
            sv = _dot(wm, vn_ref[r:r + GMLP_CHUNK, sl]) + bias
            y_ref[r:r + GMLP_CHUNK, sl] = (u[r:r + GMLP_CHUNK] * sv).astype(y_ref.dtype)


def _gmlp_in(x2, g, w_bf, ln_g, ln_b, w_sp, b_sp_tile):
    t, d = x2.shape
    width = w_bf.shape[1] // 2
    tm = GMLP_TILE
    limit = _vmem_limit(
        [_nbytes((tm, d), F32), _nbytes((tm, width), BF16)],
        [_nbytes(w_bf.shape, BF16), _nbytes(w_sp.shape, F32), _nbytes(b_sp_tile.shape, F32)],
        [_nbytes((tm, d), BF16), _nbytes((tm, width), F32), _nbytes((tm, width), BF16),
         8 * _nbytes((tm, width // GMLP_GROUPS), F32)])
    return pl.pallas_call(
        _gmlp_in_kernel,
        grid=(t // tm,),
        in_specs=[pl.BlockSpec((tm, d), lambda i: (i, 0)), _resident((1, d)),
                  _resident(w_bf.shape), _resident((1, width)), _resident((1, width)),
                  _resident(w_sp.shape), _resident(b_sp_tile.shape)],
        out_specs=pl.BlockSpec((tm, width), lambda i: (i, 0)),
        out_shape=jax.ShapeDtypeStruct((t, width), BF16),
        scratch_shapes=[pltpu.VMEM((tm, d), BF16), pltpu.VMEM((tm, width), F32),
                        pltpu.VMEM((tm, width), BF16)],
        compiler_params=pltpu.CompilerParams(
            dimension_semantics=("parallel",), vmem_limit_bytes=limit),
        name="gmlp_in",
    )(x2, g, w_bf, ln_g, ln_b, w_sp, b_sp_tile)


def _mix_ffn_kernel(n_mix, *refs):
    x_ref = refs[0]
    y_refs = refs[1:1 + n_mix]
    wm_refs = refs[1 + n_mix:1 + 2 * n_mix]
    (g_ref, wg_ref, wu_ref, dww_ref, dwb_ref, wd_ref,
     o_ref, h_ref, act_ref, carry_ref) = refs[1 + 2 * n_mix:]
    tm = x_ref.shape[0]
    d_ff = wg_ref.shape[1]
    fc = FF_CHUNK

    @pl.when(pl.program_id(1) == 0)
    def _():
        carry_ref[...] = jnp.zeros_like(carry_ref)

    x1 = x_ref[...]
    for y_ref, wm_ref in zip(y_refs, wm_refs):
        x1 = x1 + _dot(y_ref[...], wm_ref[...])
    o_ref[...] = x1
    h_ref[...] = _rms(x1, g_ref[...]).astype(BF16)

    for c in range(0, d_ff, fc):
        cols = slice(c, c + fc)
        gate = _dot(h_ref[...], wg_ref[:, cols])
        up = _dot(h_ref[...], wu_ref[:, cols])
        ext = jnp.concatenate([carry_ref[:, cols], gate], axis=0)
        carry_ref[:, cols] = gate[tm - SUBLANES:tm, :]
        conv = dwb_ref[:, cols] + dww_ref[FFN_TAPS - 1:FFN_TAPS, cols] * gate
        for k in range(FFN_TAPS - 2, -1, -1):
            ext = pltpu.roll(ext, 1, axis=0)
            conv = conv + dww_ref[k:k + 1, cols] * ext[SUBLANES:SUBLANES + tm]
        act_ref[:, cols] = (conv * jax.nn.sigmoid(conv) * up).astype(BF16)

    o_ref[...] = o_ref[...] + _dot(act_ref[...], wd_ref[...])


def _mix_ffn(x2, batch, mix_pairs, g, wg, wu, dww, dwb, wd):
    t, d = x2.shape
    d_ff = wg.shape[1]
    fc = FF_CHUNK
    n_mix = len(mix_pairs)
    ys = [y for y, _ in mix_pairs]
    wms = [w for _, w in mix_pairs]
    resident = ([_nbytes(w.shape, BF16) for w in wms] + 3 * [_nbytes(wg.shape, BF16)] +
                [_nbytes((16, d_ff), F32)])

    def blocks(rows):
        pipelined = 2 * [_nbytes((rows, d), F32)] + [_nbytes((rows, y.shape[1]), BF16) for y in ys]
        scratch = [_nbytes((rows, d), BF16), _nbytes((rows, d_ff), BF16),
                   _nbytes((SUBLANES, d_ff), F32), 4 * _nbytes((rows, fc), F32)]
        return pipelined, scratch

    tm = ROW_TILE
    for rows in (2 * ROW_TILE,):
        pipelined, scratch = blocks(rows)
        planned = 2 * sum(pipelined) + sum(resident) + sum(scratch)
        if planned <= (1.0 - FFN_VMEM_HEADROOM) * VMEM_LIMIT_CAP_BYTES and (t // batch) % rows == 0:
            tm = rows
    nt = t // batch // tm
    row = lambda width: pl.BlockSpec((tm, width), lambda b, i: (b * nt + i, 0))
    limit = _vmem_limit(*blocks(tm)[:1], resident, blocks(tm)[1])
    return pl.pallas_call(
        functools.partial(_mix_ffn_kernel, n_mix),
        grid=(batch, nt),
        in_specs=[row(d)] + [row(y.shape[1]) for y in ys] + [_resident(w.shape) for w in wms] +
                 [_resident((1, d)), _resident(wg.shape), _resident(wu.shape),
                  _resident(dww.shape), _resident(dwb.shape), _resident(wd.shape)],
        out_specs=row(d),
        out_shape=jax.ShapeDtypeStruct((t, d), F32),
        scratch_shapes=[pltpu.VMEM((tm, d), BF16), pltpu.VMEM((tm, d_ff), BF16),
                        pltpu.VMEM((SUBLANES, d_ff), F32)],
        compiler_params=pltpu.CompilerParams(
            dimension_semantics=("arbitrary", "arbitrary"), vmem_limit_bytes=limit),
        name="mix_ffn",
    )(x2, *ys, *wms, g, wg, wu, dww, dwb, wd)


def kernel(x, mix_norm_g, w_in_e, conv_w, conv_b, conv_ln_g, conv_ln_b, qn_g, kn_g, rel_bias,
           w_out_e, w_in_o, gate_ln_g, gate_ln_b, w_sp, b_sp, w_out_o, ffn_norm_g, w_gate, w_up,
           ffn_dw_w, ffn_dw_b, w_down):
    batch, seq, d = x.shape
    depth = mix_norm_g.shape[0]
    d_ff = w_gate.shape[-1]
    assert seq % ROW_TILE == 0 and seq % (ATT_SUBS * ATT_TILE) == 0 and d_ff % FF_CHUNK == 0
    assert seq % GMLP_TILE == 0 and GMLP_TILE % GMLP_CHUNK == 0
    assert seq % EVEN_TILE == 0 and seq % CONV_TILE == 0
    assert CONV_TILE % CONV_HALO == 0 and CONV_TILE % CONV_SUB == 0 and ATT_TILE % CHUNK == 0
    assert ATT_KEYS >= ATT_TILE + N_LEFT_CHUNKS * CHUNK and CONV_HALO >= CONV_TAPS - 1

    x2 = x.reshape(batch * seq, d)
    row = lambda vec: vec.reshape(1, -1).astype(F32)

    for layer in range(depth):
        i = layer // 2
        if layer % 2 == 0:
            a, q, k, v = _even_in(
                x2, row(mix_norm_g[layer]), w_in_e[i].astype(BF16),
                row(jnp.tile(qn_g[i], ATT_HEADS)), row(jnp.tile(kn_g[i], ATT_HEADS)))
            taps_w = jnp.broadcast_to(conv_w[i].astype(F32)[:, None, :],
                                      (conv_w.shape[1], SUBLANES, conv_w.shape[2]))
            a = _conv_module(a, batch, taps_w, row(conv_b[i]),
                             row(conv_ln_g[i]), row(conv_ln_b[i]))
            o = _attention(q, k, v, batch, _attn_bias_table(rel_bias[i]))
            cw = a.shape[1]
            w_out = w_out_e[i].astype(BF16)
            mix_pairs = [(a, w_out[:cw]), (o, w_out[cw:])]
        else:
            b_tile = jnp.broadcast_to(b_sp[i].astype(F32)[:, :, None],
                                      (GMLP_GROUPS, GMLP_CHUNK, LANES))
            y = _gmlp_in(x2, row(mix_norm_g[layer]), w_in_o[i].astype(BF16),
                         row(gate_ln_g[i]), row(gate_ln_b[i]), w_sp[i].astype(F32), b_tile)
            mix_pairs = [(y, w_out_o[i].astype(BF16))]
        x2 = _mix_ffn(
            x2, batch, mix_pairs, row(ffn_norm_g[layer]),
            w_gate[layer].astype(BF16), w_up[layer].astype(BF16),
            ffn_dw_w[layer].astype(F32), row(ffn_dw_b[layer]), w_down[layer].astype(BF16))
    return x2.reshape(batch, seq, d)
```

```python
import functools

import numpy as np
import jax
import jax.numpy as jnp
from jax import lax
from jax.experimental import pallas as pl
from jax.experimental.pallas import tpu as pltpu

F32 = jnp.float32
BF16 = jnp.bfloat16

EPS = 1e-6
CHUNK = 64
N_LEFT_CHUNKS = 8
REL_MAX = 4 * CHUNK
ATT_HEADS = 8
HEAD_DIM = 64
ATT_WIDTH = ATT_HEADS * HEAD_DIM
CONV_TAPS = 31
GMLP_CHUNK = 128
GMLP_GROUPS = 8
FFN_TAPS = 3
NEG_INF = -1e30
LOG2E = 1.4426950408889634

SUBLANES = 8
LANES = 128
VMEM_LIMIT_CAP_BYTES = 60000 * 1024

ROW_TILE = 512
FFN_VMEM_HEADROOM = 0.10
EVEN_TILE = 1024
CONV_TILE = 1024
GMLP_TILE = 1024
ATT_TILE = 256
ATT_KEYS = 3 * ATT_TILE
ATT_SUBS = 2
FF_CHUNK = 256
CONV_HALO = 32
CONV_SUB = 32


def _dot(a, b):
    return jnp.dot(a, b, preferred_element_type=F32)


def _dot_nt(a, b):
    return lax.dot_general(a, b, (((1,), (1,)), ((), ())), preferred_element_type=F32)


def _rms(x, g):
    return x * lax.rsqrt(jnp.mean(x * x, axis=-1, keepdims=True) + EPS) * g


def _gelu_tanh(x):
    a = float(-2.0 * np.sqrt(2.0 / np.pi) * LOG2E)
    b = a * 0.044715
    return x * (1.0 / (1.0 + jnp.exp2(x * (b * (x * x) + a))))


def _nbytes(shape, dtype):
    return int(np.prod(shape)) * jnp.dtype(dtype).itemsize


def _vmem_limit(pipelined, resident, scratch):
    need = 2 * sum(pipelined) + sum(resident) + sum(scratch)
    need += 4 * max(pipelined)
    return int(min(VMEM_LIMIT_CAP_BYTES, max(need, 16 * 1024 * 1024)))


def _resident(shape):
    nd = len(shape)
    return pl.BlockSpec(shape, lambda *_: (0,) * nd, pipeline_mode=pl.Buffered(1))


def _even_in_kernel(x_ref, g_ref, w_ref, qg_ref, kg_ref,
                    a_ref, q_ref, k_ref, v_ref, h_ref):
    cw = a_ref.shape[-1]
    x = x_ref[...]
    h_ref[...] = _rms(x, g_ref[...]).astype(BF16)

    a_lin = _dot(h_ref[...], w_ref[:, 0:cw])
    a_gate = _dot(h_ref[...], w_ref[:, cw:2 * cw])
    a_ref[...] = a_lin * jax.nn.sigmoid(a_gate)

    def head_norm(z, gain):
        s = z * z
        low = lax.broadcasted_iota(jnp.int32, (1, 2 * HEAD_DIM), 1) < HEAD_DIM
        parts = []
        for c in range(0, s.shape[1], 2 * HEAD_DIM):
            sg = s[:, c:c + 2 * HEAD_DIM]
            s_low = jnp.sum(jnp.where(low, sg, 0.0), axis=-1, keepdims=True)
            s_high = jnp.sum(jnp.where(low, 0.0, sg), axis=-1, keepdims=True)
            parts.append(jnp.where(low, s_low, s_high))
        ss = jnp.concatenate(parts, axis=1)
        return z * lax.rsqrt(ss * (1.0 / HEAD_DIM) + EPS) * gain

    base = 2 * cw
    q = _dot(h_ref[...], w_ref[:, base:base + ATT_WIDTH])
    q_ref[...] = (head_norm(q, qg_ref[...]) * (HEAD_DIM ** -0.5 * LOG2E)).astype(BF16)
    k = _dot(h_ref[...], w_ref[:, base + ATT_WIDTH:base + 2 * ATT_WIDTH])
    k_ref[...] = head_norm(k, kg_ref[...]).astype(BF16)
    v = _dot(h_ref[...], w_ref[:, base + 2 * ATT_WIDTH:base + 3 * ATT_WIDTH])
    v_ref[...] = v.astype(BF16)


def _even_in(x2, g, w_bf, qg, kg):
    t, d = x2.shape
    n = w_bf.shape[1]
    cw = (n - 3 * ATT_WIDTH) // 2
    tm = EVEN_TILE
    row = lambda width: pl.BlockSpec((tm, width), lambda i: (i, 0))
    limit = _vmem_limit(
        [_nbytes((tm, d), F32), _nbytes((tm, cw), F32)] + 3 * [_nbytes((tm, ATT_WIDTH), BF16)],
        [_nbytes(w_bf.shape, BF16)],
        [_nbytes((tm, d), BF16), 4 * _nbytes((tm, 2 * cw), F32)])
    return pl.pallas_call(
        _even_in_kernel,
        grid=(t // tm,),
        in_specs=[row(d), _resident((1, d)), _resident(w_bf.shape),
                  _resident((1, ATT_WIDTH)), _resident((1, ATT_WIDTH))],
        out_specs=[row(cw), row(ATT_WIDTH), row(ATT_WIDTH), row(ATT_WIDTH)],
        out_shape=[jax.ShapeDtypeStruct((t, cw), F32)] +
                  3 * [jax.ShapeDtypeStruct((t, ATT_WIDTH), BF16)],
        scratch_shapes=[pltpu.VMEM((tm, d), BF16)],
        compiler_params=pltpu.CompilerParams(
            dimension_semantics=("parallel",), vmem_limit_bytes=limit),
        name="even_in",
    )(x2, g, w_bf, qg, kg)


def _conv_module_kernel(prev_ref, cur_ref, w_ref, b_ref, lg_ref, lb_ref, o_ref, buf_ref, sh_ref):
    ts = cur_ref.shape[0]
    halo = prev_ref.shape[0]
    taps = w_ref.shape[0]
    span = sh_ref.shape[1]
    first = pl.program_id(1) == 0
    buf_ref[0:halo, :] = jnp.where(first, 0.0, prev_ref[...])
    buf_ref[halo:halo + ts, :] = cur_ref[...]
    for s in range(1, SUBLANES):
        for r in range(0, span, CONV_SUB):
            n = min(CONV_SUB, span - r)
            sh_ref[s - 1, r:r + n, :] = buf_ref[r + s:r + s + n, :]
    off = halo - (taps - 1)
    taps_w = [jnp.concatenate([w_ref[k]] * (CONV_SUB // SUBLANES), axis=0) for k in range(taps)]
    for r in range(0, ts, CONV_SUB):
        acc = jnp.zeros((CONV_SUB, cur_ref.shape[1]), F32) + b_ref[...]
        for k in range(taps):
            s = (off + k) % SUBLANES
            lo = r + off + k - s
            rows = buf_ref[lo:lo + CONV_SUB, :] if s == 0 else sh_ref[s - 1, lo:lo + CONV_SUB, :]
            acc = acc + taps_w[k] * rows
        mu = jnp.mean(acc, axis=-1, keepdims=True)
        cen = acc - mu
        var = jnp.mean(cen * cen, axis=-1, keepdims=True)
        y = cen * lax.rsqrt(var + EPS) * lg_ref[...] + lb_ref[...]
        o_ref[r:r + CONV_SUB, :] = (y * jax.nn.sigmoid(y)).astype(o_ref.dtype)


def _conv_module(a2, batch, conv_w, conv_b, ln_g, ln_b):
    t, c = a2.shape
    ts = CONV_TILE
    nt = t // batch // ts
    per = ts // CONV_HALO
    shifted = (SUBLANES - 1, ts + CONV_HALO - SUBLANES, c)
    limit = _vmem_limit([_nbytes((ts, c), F32), _nbytes((CONV_HALO, c), F32),
                         _nbytes((ts, c), BF16)],
                        [_nbytes(conv_w.shape, F32)],
                        [_nbytes((ts + CONV_HALO, c), F32), _nbytes(shifted, F32)])
    return pl.pallas_call(
        _conv_module_kernel,
        grid=(batch, nt),
        in_specs=[
            pl.BlockSpec((CONV_HALO, c), lambda b, i: (jnp.maximum((b * nt + i) * per - 1, 0), 0)),
            pl.BlockSpec((ts, c), lambda b, i: (b * nt + i, 0)),
            _resident(conv_w.shape), _resident((1, c)), _resident((1, c)), _resident((1, c))],
        out_specs=pl.BlockSpec((ts, c), lambda b, i: (b * nt + i, 0)),
        out_shape=jax.ShapeDtypeStruct((t, c), BF16),
        scratch_shapes=[pltpu.VMEM((ts + CONV_HALO, c), F32), pltpu.VMEM(shifted, F32)],
        compiler_params=pltpu.CompilerParams(
            dimension_semantics=("parallel", "parallel"), vmem_limit_bytes=limit),
        name="conv_module",
    )(a2, a2, conv_w, conv_b, ln_g, ln_b)


def _attn_kernel(*refs):
    n_kv = ATT_SUBS + 2
    q_ref, k_refs, v_refs = refs[0], refs[1:1 + n_kv], refs[1 + n_kv:1 + 2 * n_kv]
    bias_refs, o_ref = refs[1 + 2 * n_kv:-1], refs[-1]
    low = lax.broadcasted_iota(jnp.int32, (1, 2 * HEAD_DIM), 1) < HEAD_DIM
    keep = [jnp.where(low, 1.0, 0.0).astype(BF16), jnp.where(low, 0.0, 1.0).astype(BF16)]
    tq = ATT_TILE
    for j in range(ATT_SUBS):
        rows = slice(j * tq, (j + 1) * tq)
        for p in range(ATT_HEADS // 2):
            cols = slice(2 * HEAD_DIM * p, 2 * HEAD_DIM * (p + 1))
            qp = q_ref[rows, cols]
            kp = jnp.concatenate([r[:, cols] for r in k_refs[j:j + 3]], axis=0)
            vp = jnp.concatenate([r[:, cols] for r in v_refs[j:j + 3]], axis=0)
            q2 = jnp.concatenate([qp * keep[0], qp * keep[1]], axis=0)
            bias_ref = bias_refs[min(j, len(bias_refs) - 1)]
            bias = bias_ref[0, 2 * p:2 * p + 2].reshape(2 * tq, kp.shape[0])
            s = _dot_nt(q2, kp) + bias
            m = jnp.max(s, axis=-1, keepdims=True)
            pe = jnp.exp2(s - m)
            den = jnp.sum(pe, axis=-1, keepdims=True)
            o2 = _dot(pe.astype(BF16), vp) * (1.0 / den)
            o_ref[rows, cols] = jnp.where(low, o2[0:tq], o2[tq:2 * tq]).astype(o_ref.dtype)


def _attn_bias_table(rel_bias):
    heads = rel_bias.shape[0]
    band_keys = (N_LEFT_CHUNKS + 1) * CHUNK
    period = CHUNK + band_keys
    delta = np.arange(period)
    delta = np.where(delta < band_keys, delta, delta - period)
    idx = np.clip(N_LEFT_CHUNKS * CHUNK - delta, -(CHUNK - 1), REL_MAX) + (CHUNK - 1)
    one_period = rel_bias[:, idx].astype(F32) * LOG2E
    flat = jnp.tile(one_period, (1, CHUNK))[:, :CHUNK * (period - 1)]
    block = flat.reshape(heads, CHUNK, period - 1)[:, :, :band_keys]
    rows = [jnp.pad(block, ((0, 0), (0, 0), (j * CHUNK, ATT_KEYS - band_keys - j * CHUNK)),
                    constant_values=NEG_INF) for j in range(ATT_TILE // CHUNK)]
    table = jnp.concatenate(rows, axis=1)
    kk = np.arange(ATT_KEYS)[None, :]
    present = np.stack([kk >= (2 - v) * ATT_TILE for v in range(3)])
    return jnp.where(present[:, None], table[None], NEG_INF)


def _attention(q2, k2, v2, batch, bias_tab):
    t, w = q2.shape
    tq = ATT_TILE
    step = ATT_SUBS * tq
    nt = t // batch // step
    n_kv = ATT_SUBS + 2
    q_blk = pl.BlockSpec((step, w), lambda b, i: (b * nt + i, 0))
    kv_blk = lambda m: pl.BlockSpec(
        (tq, w), lambda b, i: (b * nt * ATT_SUBS + jnp.maximum(ATT_SUBS * i - 2 + m, 0), 0))
    n_bias = min(ATT_SUBS, 3)
    bias_blk = lambda j: pl.BlockSpec(
        (1,) + bias_tab.shape[1:], lambda b, i: (jnp.minimum(ATT_SUBS * i + j, 2), 0, 0, 0),
        pipeline_mode=pl.Buffered(1))
    limit = _vmem_limit([2 * _nbytes((step, w), BF16)] + 2 * n_kv * [_nbytes((tq, w), BF16)],
                        n_bias * [_nbytes(bias_tab.shape[1:], F32)],
                        [8 * _nbytes((2 * tq, ATT_KEYS), F32)])
    return pl.pallas_call(
        _attn_kernel,
        grid=(batch, nt),
        in_specs=[q_blk] + 2 * [kv_blk(m) for m in range(n_kv)] +
                 [bias_blk(j) for j in range(n_bias)],
        out_specs=q_blk,
        out_shape=jax.ShapeDtypeStruct((t, w), BF16),
        compiler_params=pltpu.CompilerParams(
            dimension_semantics=("parallel", "parallel"), vmem_limit_bytes=limit),
        name="chunk_attention",
    )(q2, *(n_kv * [k2]), *(n_kv * [v2]), *(n_bias * [bias_tab]))


def _gmlp_in_kernel(x_ref, g_ref, w_ref, lg_ref, lb_ref, wsp_ref, bsp_ref, y_ref,
                    h_ref, v_ref, vn_ref):
    tm = x_ref.shape[0]
    width = v_ref.shape[1]
    gd = width // GMLP_GROUPS
    h_ref[...] = _rms(x_ref[...], g_ref[...]).astype(BF16)

    s1 = jnp.zeros((tm, 1), F32)
    s2 = jnp.zeros((tm, 1), F32)
    for g in range(GMLP_GROUPS):
        zv = _gelu_tanh(_dot(h_ref[...], w_ref[:, width + g * gd:width + (g + 1) * gd]))
        v_ref[:, g * gd:(g + 1) * gd] = zv
        if g == 0:
            pivot = jnp.mean(zv, axis=-1, keepdims=True)
        dev = zv - pivot
        s1 = s1 + jnp.sum(dev, axis=-1, keepdims=True)
        s2 = s2 + jnp.sum(dev * dev, axis=-1, keepdims=True)
    m1 = s1 * (1.0 / width)
    mu = pivot + m1
    rstd = lax.rsqrt(s2 * (1.0 / width) - m1 * m1 + EPS)
    for g in range(GMLP_GROUPS):
        sl = slice(g * gd, (g + 1) * gd)
        vn_ref[:, sl] = ((v_ref[:, sl] - mu) * rstd * lg_ref[:, sl] + lb_ref[:, sl]).astype(BF16)

    row = lax.broadcasted_iota(jnp.int32, (GMLP_CHUNK, GMLP_CHUNK), 0)
    col = lax.broadcasted_iota(jnp.int32, (GMLP_CHUNK, GMLP_CHUNK), 1)
    causal = row >= col
    for g in range(GMLP_GROUPS):
        sl = slice(g * gd, (g + 1) * gd)
        u = _gelu_tanh(_dot(h_ref[...], w_ref[:, sl]))
        wm = jnp.where(causal, wsp_ref[g], 0.0).astype(BF16)
        bias = jnp.concatenate([bsp_ref[g]] * (gd // LANES), axis=-1)
        for r in range(0, tm, GMLP_CHUNK):
            sv = _dot(wm, vn_ref[r:r + GMLP_CHUNK, sl]) + bias
            y_ref[r:r + GMLP_CHUNK, sl] = (u[r:r + GMLP_CHUNK] * sv).astype(y_ref.dtype)


def _gmlp_in(x2, g, w_bf, ln_g, ln_b, w_sp, b_sp_tile):
    t, d = x2.shape
    width = w_bf.shape[1] // 2
    tm = GMLP_TILE
    limit = _vmem_limit(
        [_nbytes((tm, d), F32), _nbytes((tm, width), BF16)],
        [_nbytes(w_bf.shape, BF16), _nbytes(w_sp.shape, F32), _nbytes(b_sp_tile.shape, F32)],
        [_nbytes((tm, d), BF16), _nbytes((tm, width), F32), _nbytes((tm, width), BF16),
         8 * _nbytes((tm, width // GMLP_GROUPS), F32)])
    return pl.pallas_call(
        _gmlp_in_kernel,
        grid=(t // tm,),
        in_specs=[pl.BlockSpec((tm, d), lambda i: (i, 0)), _resident((1, d)),
                  _resident(w_bf.shape), _resident((1, width)), _resident((1, width)),
                  _resident(w_sp.shape), _resident(b_sp_tile.shape)],
        out_specs=pl.BlockSpec((tm, width), lambda i: (i, 0)),
        out_shape=jax.ShapeDtypeStruct((t, width), BF16),
        scratch_shapes=[pltpu.VMEM((tm, d), BF16), pltpu.VMEM((tm, width), F32),
                        pltpu.VMEM((tm, width), BF16)],
        compiler_params=pltpu.CompilerParams(
            dimension_semantics=("parallel",), vmem_limit_bytes=limit),
        name="gmlp_in",
    )(x2, g, w_bf, ln_g, ln_b, w_sp, b_sp_tile)


def _mix_ffn_kernel(n_mix, *refs):
    x_ref = refs[0]
    y_refs = refs[1:1 + n_mix]
    wm_refs = refs[1 + n_mix:1 + 2 * n_mix]
    (g_ref, wg_ref, wu_ref, dww_ref, dwb_ref, wd_ref,
     o_ref, h_ref, act_ref, carry_ref) = refs[1 + 2 * n_mix:]
    tm = x_ref.shape[0]
    d_ff = wg_ref.shape[1]
    fc = FF_CHUNK

    @pl.when(pl.program_id(1) == 0)
    def _():
        carry_ref[...] = jnp.zeros_like(carry_ref)

    x1 = x_ref[...]
    for y_ref, wm_ref in zip(y_refs, wm_refs):
        x1 = x1 + _dot(y_ref[...], wm_ref[...])
    o_ref[...] = x1
    h_ref[...] = _rms(x1, g_ref[...]).astype(BF16)

    for c in range(0, d_ff, fc):
        cols = slice(c, c + fc)
        gate = _dot(h_ref[...], wg_ref[:, cols])
        up = _dot(h_ref[...], wu_ref[:, cols])
        ext = jnp.concatenate([carry_ref[:, cols], gate], axis=0)
        carry_ref[:, cols] = gate[tm - SUBLANES:tm, :]
        conv = dwb_ref[:, cols] + dww_ref[FFN_TAPS - 1:FFN_TAPS, cols] * gate
        for k in range(FFN_TAPS - 2, -1, -1):
            ext = pltpu.roll(ext, 1, axis=0)
            conv = conv + dww_ref[k:k + 1, cols] * ext[SUBLANES:SUBLANES + tm]
        act_ref[:, cols] = (conv * jax.nn.sigmoid(conv) * up).astype(BF16)

    o_ref[...] = o_ref[...] + _dot(act_ref[...], wd_ref[...])


def _mix_ffn(x2, batch, mix_pairs, g, wg, wu, dww, dwb, wd):
    t, d = x2.shape
    d_ff = wg.shape[1]
    fc = FF_CHUNK
    n_mix = len(mix_pairs)
    ys = [y for y, _ in mix_pairs]
    wms = [w for _, w in mix_pairs]
    resident = ([_nbytes(w.shape, BF16) for w in wms] + 3 * [_nbytes(wg.shape, BF16)] +
                [_nbytes((16, d_ff), F32)])

    def blocks(rows):
        pipelined = 2 * [_nbytes((rows, d), F32)] + [_nbytes((rows, y.shape[1]), BF16) for y in ys]
        scratch = [_nbytes((rows, d), BF16), _nbytes((rows, d_ff), BF16),
                   _nbytes((SUBLANES, d_ff), F32), 4 * _nbytes((rows, fc), F32)]
        return pipelined, scratch

    tm = ROW_TILE
    for rows in (2 * ROW_TILE,):
        pipelined, scratch = blocks(rows)
        planned = 2 * sum(pipelined) + sum(resident) + sum(scratch)
        if planned <= (1.0 - FFN_VMEM_HEADROOM) * VMEM_LIMIT_CAP_BYTES and (t // batch) % rows == 0:
            tm = rows
    nt = t // batch // tm
    row = lambda width: pl.BlockSpec((tm, width), lambda b, i: (b * nt + i, 0))
    limit = _vmem_limit(*blocks(tm)[:1], resident, blocks(tm)[1])
    return pl.pallas_call(
        functools.partial(_mix_ffn_kernel, n_mix),
        grid=(batch, nt),
        in_specs=[row(d)] + [row(y.shape[1]) for y in ys] + [_resident(w.shape) for w in wms] +
                 [_resident((1, d)), _resident(wg.shape), _resident(wu.shape),
                  _resident(dww.shape), _resident(dwb.shape), _resident(wd.shape)],
        out_specs=row(d),
        out_shape=jax.ShapeDtypeStruct((t, d), F32),
        scratch_shapes=[pltpu.VMEM((tm, d), BF16), pltpu.VMEM((tm, d_ff), BF16),
                        pltpu.VMEM((SUBLANES, d_ff), F32)],
        compiler_params=pltpu.CompilerParams(
            dimension_semantics=("arbitrary", "arbitrary"), vmem_limit_bytes=limit),
        name="mix_ffn",
    )(x2, *ys, *wms, g, wg, wu, dww, dwb, wd)


def kernel(x, mix_norm_g, w_in_e, conv_w, conv_b, conv_ln_g, conv_ln_b, qn_g, kn_g, rel_bias,
           w_out_e, w_in_o, gate_ln_g, gate_ln_b, w_sp, b_sp, w_out_o, ffn_norm_g, w_gate, w_up,
           ffn_dw_w, ffn_dw_b, w_down):
    batch, seq, d = x.shape
    depth = mix_norm_g.shape[0]
    d_ff = w_gate.shape[-1]
    assert seq % ROW_TILE == 0 and seq % (ATT_SUBS * ATT_TILE) == 0 and d_ff % FF_CHUNK == 0
    assert seq % GMLP_TILE == 0 and GMLP_TILE % GMLP_CHUNK == 0
    assert seq % EVEN_TILE == 0 and seq % CONV_TILE == 0
    assert CONV_TILE % CONV_HALO == 0 and CONV_TILE % CONV_SUB == 0 and ATT_TILE % CHUNK == 0
    assert ATT_KEYS >= ATT_TILE + N_LEFT_CHUNKS * CHUNK and CONV_HALO >= CONV_TAPS - 1

    x2 = x.reshape(batch * seq, d)
    row = lambda vec: vec.reshape(1, -1).astype(F32)

    for layer in range(depth):
        i = layer // 2
        if layer % 2 == 0:
            a, q, k, v = _even_in(
                x2, row(mix_norm_g[layer]), w_in_e[i].astype(BF16),
                row(jnp.tile(qn_g[i], ATT_HEADS)), row(jnp.tile(kn_g[i], ATT_HEADS)))
            taps_w = jnp.broadcast_to(conv_w[i].astype(F32)[:, None, :],
                                      (conv_w.shape[1], SUBLANES, conv_w.shape[2]))
            a = _conv_module(a, batch, taps_w, row(conv_b[i]),
                             row(conv_ln_g[i]), row(conv_ln_b[i]))
            o = _attention(q, k, v, batch, _attn_bias_table(rel_bias[i]))
            cw = a.shape[1]
            w_out = w_out_e[i].astype(BF16)
            mix_pairs = [(a, w_out[:cw]), (o, w_out[cw:])]
        else:
            b_tile = jnp.broadcast_to(b_sp[i].astype(F32)[:, :, None],
                                      (GMLP_GROUPS, GMLP_CHUNK, LANES))
            y = _gmlp_in(x2, row(mix_norm_g[layer]), w_in_o[i].astype(BF16),
                         row(gate_ln_g[i]), row(gate_ln_b[i]), w_sp[i].astype(F32), b_tile)
            mix_pairs = [(y, w_out_o[i].astype(BF16))]
        x2 = _mix_ffn(
            x2, batch, mix_pairs, row(ffn_norm_g[layer]),
            w_gate[layer].astype(BF16), w_up[layer].astype(BF16),
            ffn_dw_w[layer].astype(F32), row(ffn_dw_b[layer]), w_down[layer].astype(BF16))
    return x2.reshape(batch, seq, d)
```

```python
import functools

import numpy as np
import jax
import jax.numpy as jnp
from jax import lax
from jax.experimental import pallas as pl
from jax.experimental.pallas import tpu as pltpu

F32 = jnp.float32
BF16 = jnp.bfloat16

EPS = 1e-6
CHUNK = 64
N_LEFT_CHUNKS = 8
REL_MAX = 4 * CHUNK
ATT_HEADS = 8
HEAD_DIM = 64
ATT_WIDTH = ATT_HEADS * HEAD_DIM
CONV_TAPS = 31
GMLP_CHUNK = 128
GMLP_GROUPS = 8
FFN_TAPS = 3
NEG_INF = -1e30
LOG2E = 1.4426950408889634

SUBLANES = 8
LANES = 128
VMEM_LIMIT_CAP_BYTES = 60000 * 1024

ROW_TILE = 512
FFN_VMEM_HEADROOM = 0.10
EVEN_TILE = 1024
CONV_TILE = 1024
GMLP_TILE = 1024
ATT_TILE = 256
ATT_KEYS = 3 * ATT_TILE
ATT_SUBS = 2
FF_CHUNK = 256
CONV_HALO = 32
CONV_SUB = 32


def _dot(a, b):
    return jnp.dot(a, b, preferred_element_type=F32)


def _dot_nt(a, b):
    return lax.dot_general(a, b, (((1,), (1,)), ((), ())), preferred_element_type=F32)


def _rms(x, g):
    return x * lax.rsqrt(jnp.mean(x * x, axis=-1, keepdims=True) + EPS) * g


def _gelu_tanh(x):
    a = float(-2.0 * np.sqrt(2.0 / np.pi) * LOG2E)
    b = a * 0.044715
    return x * (1.0 / (1.0 + jnp.exp2(x * (b * (x * x) + a))))


def _nbytes(shape, dtype):
    return int(np.prod(shape)) * jnp.dtype(dtype).itemsize


def _vmem_limit(pipelined, resident, scratch):
    need = 2 * sum(pipelined) + sum(resident) + sum(scratch)
    need += 4 * max(pipelined)
    return int(min(VMEM_LIMIT_CAP_BYTES, max(need, 16 * 1024 * 1024)))


def _resident(shape):
    nd = len(shape)
    return pl.BlockSpec(shape, lambda *_: (0,) * nd, pipeline_mode=pl.Buffered(1))


def _even_in_kernel(x_ref, g_ref, w_ref, qg_ref, kg_ref,
                    a_ref, q_ref, k_ref, v_ref, h_ref):
    cw = a_ref.shape[-1]
    x = x_ref[...]
    h_ref[...] = _rms(x, g_ref[...]).astype(BF16)

    a_lin = _dot(h_ref[...], w_ref[:, 0:cw])
    a_gate = _dot(h_ref[...], w_ref[:, cw:2 * cw])
    a_ref[...] = a_lin * jax.nn.sigmoid(a_gate)

    def head_norm(z, gain):
        s = z * z
        low = lax.broadcasted_iota(jnp.int32, (1, 2 * HEAD_DIM), 1) < HEAD_DIM
        parts = []
        for c in range(0, s.shape[1], 2 * HEAD_DIM):
            sg = s[:, c:c + 2 * HEAD_DIM]
            s_low = jnp.sum(jnp.where(low, sg, 0.0), axis=-1, keepdims=True)
            s_high = jnp.sum(jnp.where(low, 0.0, sg), axis=-1, keepdims=True)
            parts.append(jnp.where(low, s_low, s_high))
        ss = jnp.concatenate(parts, axis=1)
        return z * lax.rsqrt(ss * (1.0 / HEAD_DIM) + EPS) * gain

    base = 2 * cw
    q = _dot(h_ref[...], w_ref[:, base:base + ATT_WIDTH])
    q_ref[...] = (head_norm(q, qg_ref[...]) * (HEAD_DIM ** -0.5 * LOG2E)).astype(BF16)
    k = _dot(h_ref[...], w_ref[:, base + ATT_WIDTH:base + 2 * ATT_WIDTH])
    k_ref[...] = head_norm(k, kg_ref[...]).astype(BF16)
    v = _dot(h_ref[...], w_ref[:, base + 2 * ATT_WIDTH:base + 3 * ATT_WIDTH])
    v_ref[...] = v.astype(BF16)


def _even_in(x2, g, w_bf, qg, kg):
    t, d = x2.shape
    n = w_bf.shape[1]
    cw = (n - 3 * ATT_WIDTH) // 2
    tm = EVEN_TILE
    row = lambda width: pl.BlockSpec((tm, width), lambda i: (i, 0))
    limit = _vmem_limit(
        [_nbytes((tm, d), F32), _nbytes((tm, cw), F32)] + 3 * [_nbytes((tm, ATT_WIDTH), BF16)],
        [_nbytes(w_bf.shape, BF16)],
        [_nbytes((tm, d), BF16), 4 * _nbytes((tm, 2 * cw), F32)])
    return pl.pallas_call(
        _even_in_kernel,
        grid=(t // tm,),
        in_specs=[row(d), _resident((1, d)), _resident(w_bf.shape),
                  _resident((1, ATT_WIDTH)), _resident((1, ATT_WIDTH))],
        out_specs=[row(cw), row(ATT_WIDTH), row(ATT_WIDTH), row(ATT_WIDTH)],
        out_shape=[jax.ShapeDtypeStruct((t, cw), F32)] +
                  3 * [jax.ShapeDtypeStruct((t, ATT_WIDTH), BF16)],
        scratch_shapes=[pltpu.VMEM((tm, d), BF16)],
        compiler_params=pltpu.CompilerParams(
            dimension_semantics=("parallel",), vmem_limit_bytes=limit),
        name="even_in",
    )(x2, g, w_bf, qg, kg)


def _conv_module_kernel(prev_ref, cur_ref, w_ref, b_ref, lg_ref, lb_ref, o_ref, buf_ref, sh_ref):
    ts = cur_ref.shape[0]
    halo = prev_ref.shape[0]
    taps = w_ref.shape[0]
    span = sh_ref.shape[1]
    first = pl.program_id(1) == 0
    buf_ref[0:halo, :] = jnp.where(first, 0.0, prev_ref[...])
    buf_ref[halo:halo + ts, :] = cur_ref[...]
    for s in range(1, SUBLANES):
        for r in range(0, span, CONV_SUB):
            n = min(CONV_SUB, span - r)
            sh_ref[s - 1, r:r + n, :] = buf_ref[r + s:r + s + n, :]
    off = halo - (taps - 1)
    taps_w = [jnp.concatenate([w_ref[k]] * (CONV_SUB // SUBLANES), axis=0) for k in range(taps)]
    for r in range(0, ts, CONV_SUB):
        acc = jnp.zeros((CONV_SUB, cur_ref.shape[1]), F32) + b_ref[...]
        for k in range(taps):
            s = (off + k) % SUBLANES
            lo = r + off + k - s
            rows = buf_ref[lo:lo + CONV_SUB, :] if s == 0 else sh_ref[s - 1, lo:lo + CONV_SUB, :]
            acc = acc + taps_w[k] * rows
        mu = jnp.mean(acc, axis=-1, keepdims=True)
        cen = acc - mu
        var = jnp.mean(cen * cen, axis=-1, keepdims=True)
        y = cen * lax.rsqrt(var + EPS) * lg_ref[...] + lb_ref[...]
        o_ref[r:r + CONV_SUB, :] = (y * jax.nn.sigmoid(y)).astype(o_ref.dtype)


def _conv_module(a2, batch, conv_w, conv_b, ln_g, ln_b):
    t, c = a2.shape
    ts = CONV_TILE
    nt = t // batch // ts
    per = ts // CONV_HALO
    shifted = (SUBLANES - 1, ts + CONV_HALO - SUBLANES, c)
    limit = _vmem_limit([_nbytes((ts, c), F32), _nbytes((CONV_HALO, c), F32),
                         _nbytes((ts, c), BF16)],
                        [_nbytes(conv_w.shape, F32)],
                        [_nbytes((ts + CONV_HALO, c), F32), _nbytes(shifted, F32)])
    return pl.pallas_call(
        _conv_module_kernel,
        grid=(batch, nt),
        in_specs=[
            pl.BlockSpec((CONV_HALO, c), lambda b, i: (jnp.maximum((b * nt + i) * per - 1, 0), 0)),
            pl.BlockSpec((ts, c), lambda b, i: (b * nt + i, 0)),
            _resident(conv_w.shape), _resident((1, c)), _resident((1, c)), _resident((1, c))],
        out_specs=pl.BlockSpec((ts, c), lambda b, i: (b * nt + i, 0)),
        out_shape=jax.ShapeDtypeStruct((t, c), BF16),
        scratch_shapes=[pltpu.VMEM((ts + CONV_HALO, c), F32), pltpu.VMEM(shifted, F32)],
        compiler_params=pltpu.CompilerParams(
            dimension_semantics=("parallel", "parallel"), vmem_limit_bytes=limit),
        name="conv_module",
    )(a2, a2, conv_w, conv_b, ln_g, ln_b)


def _attn_kernel(*refs):
    n_kv = ATT_SUBS + 2
    q_ref, k_refs, v_refs = refs[0], refs[1:1 + n_kv], refs[1 + n_kv:1 + 2 * n_kv]
    bias_refs, o_ref = refs[1 + 2 * n_kv:-1], refs[-1]
    low = lax.broadcasted_iota(jnp.int32, (1, 2 * HEAD_DIM), 1) < HEAD_DIM
    keep = [jnp.where(low, 1.0, 0.0).astype(BF16), jnp.where(low, 0.0, 1.0).astype(BF16)]
    tq = ATT_TILE
    for j in range(ATT_SUBS):
        rows = slice(j * tq, (j + 1) * tq)
        for p in range(ATT_HEADS // 2):
            cols = slice(2 * HEAD_DIM * p, 2 * HEAD_DIM * (p + 1))
            qp = q_ref[rows, cols]
            kp = jnp.concatenate([r[:, cols] for r in k_refs[j:j + 3]], axis=0)
            vp = jnp.concatenate([r[:, cols] for r in v_refs[j:j + 3]], axis=0)
            q2 = jnp.concatenate([qp * keep[0], qp * keep[1]], axis=0)
            bias_ref = bias_refs[min(j, len(bias_refs) - 1)]
            bias = bias_ref[0, 2 * p:2 * p + 2].reshape(2 * tq, kp.shape[0])
            s = _dot_nt(q2, kp) + bias
            m = jnp.max(s, axis=-1, keepdims=True)
            pe = jnp.exp2(s - m)
            den = jnp.sum(pe, axis=-1, keepdims=True)
            o2 = _dot(pe.astype(BF16), vp) * (1.0 / den)
            o_ref[rows, cols] = jnp.where(low, o2[0:tq], o2[tq:2 * tq]).astype(o_ref.dtype)


def _attn_bias_table(rel_bias):
    heads = rel_bias.shape[0]
    band_keys = (N_LEFT_CHUNKS + 1) * CHUNK
    period = CHUNK + band_keys
    delta = np.arange(period)
    delta = np.where(delta < band_keys, delta, delta - period)
    idx = np.clip(N_LEFT_CHUNKS * CHUNK - delta, -(CHUNK - 1), REL_MAX) + (CHUNK - 1)
    one_period = rel_bias[:, idx].astype(F32) * LOG2E
    flat = jnp.tile(one_period, (1, CHUNK))[:, :CHUNK * (period - 1)]
    block = flat.reshape(heads, CHUNK, period - 1)[:, :, :band_keys]
    rows = [jnp.pad(block, ((0, 0), (0, 0), (j * CHUNK, ATT_KEYS - band_keys - j * CHUNK)),
                    constant_values=NEG_INF) for j in range(ATT_TILE // CHUNK)]
    table = jnp.concatenate(rows, axis=1)
    kk = np.arange(ATT_KEYS)[None, :]
    present = np.stack([kk >= (2 - v) * ATT_TILE for v in range(3)])
    return jnp.where(present[:, None], table[None], NEG_INF)


def _attention(q2, k2, v2, batch, bias_tab):
    t, w = q2.shape
    tq = ATT_TILE
    step = ATT_SUBS * tq
    nt = t // batch // step
    n_kv = ATT_SUBS + 2
    q_blk = pl.BlockSpec((step, w), lambda b, i: (b * nt + i, 0))
    kv_blk = lambda m: pl.BlockSpec(
        (tq, w), lambda b, i: (b * nt * ATT_SUBS + jnp.maximum(ATT_SUBS * i - 2 + m, 0), 0))
    n_bias = min(ATT_SUBS, 3)
    bias_blk = lambda j: pl.BlockSpec(
        (1,) + bias_tab.shape[1:], lambda b, i: (jnp.minimum(ATT_SUBS * i + j, 2), 0, 0, 0),
        pipeline_mode=pl.Buffered(1))
    limit = _vmem_limit([2 * _nbytes((step, w), BF16)] + 2 * n_kv * [_nbytes((tq, w), BF16)],
                        n_bias * [_nbytes(bias_tab.shape[1:], F32)],
                        [8 * _nbytes((2 * tq, ATT_KEYS), F32)])
    return pl.pallas_call(
        _attn_kernel,
        grid=(batch, nt),
        in_specs=[q_blk] + 2 * [kv_blk(m) for m in range(n_kv)] +
                 [bias_blk(j) for j in range(n_bias)],
        out_specs=q_blk,
        out_shape=jax.ShapeDtypeStruct((t, w), BF16),
        compiler_params=pltpu.CompilerParams(
            dimension_semantics=("parallel", "parallel"), vmem_limit_bytes=limit),
        name="chunk_attention",
    )(q2, *(n_kv * [k2]), *(n_kv * [v2]), *(n_bias * [bias_tab]))


def _gmlp_in_kernel(x_ref, g_ref, w_ref, lg_ref, lb_ref, wsp_ref, bsp_ref, y_ref,
                    h_ref, v_ref, vn_ref):
    tm = x_ref.shape[0]
    width = v_ref.shape[1]
    gd = width // GMLP_GROUPS
    h_ref[...] = _rms(x_ref[...], g_ref[...]).astype(BF16)

    s1 = jnp.zeros((tm, 1), F32)
    s2 = jnp.zeros((tm, 1), F32)
    for g in range(GMLP_GROUPS):
        zv = _gelu_tanh(_dot(h_ref[...], w_ref[:, width + g * gd:width + (g + 1) * gd]))
        v_ref[:, g * gd:(g + 1) * gd] = zv
        if g == 0:
            pivot = jnp.mean(zv, axis=-1, keepdims=True)
        dev = zv - pivot
        s1 = s1 + jnp.sum(dev, axis=-1, keepdims=True)
        s2 = s2 + jnp.sum(dev * dev, axis=-1, keepdims=True)
    m1 = s1 * (1.0 / width)
    mu = pivot + m1
    rstd = lax.rsqrt(s2 * (1.0 / width) - m1 * m1 + EPS)
    for g in range(GMLP_GROUPS):
        sl = slice(g * gd, (g + 1) * gd)
        vn_ref[:, sl] = ((v_ref[:, sl] - mu) * rstd * lg_ref[:, sl] + lb_ref[:, sl]).astype(BF16)

    row = lax.broadcasted_iota(jnp.int32, (GMLP_CHUNK, GMLP_CHUNK), 0)
    col = lax.broadcasted_iota(jnp.int32, (GMLP_CHUNK, GMLP_CHUNK), 1)
    causal = row >= col
    for g in range(GMLP_GROUPS):
        sl = slice(g * gd, (g + 1) * gd)
        u = _gelu_tanh(_dot(h_ref[...], w_ref[:, sl]))
        wm = jnp.where(causal, wsp_ref[g], 0.0).astype(BF16)
        bias = jnp.concatenate([bsp_ref[g]] * (gd // LANES), axis=-1)
        for r in range(0, tm, GMLP_CHUNK):
            sv = _dot(wm, vn_ref[r:r + GMLP_CHUNK, sl]) + bias
            y_ref[r:r + GMLP_CHUNK, sl] = (u[r:r + GMLP_CHUNK] * sv).astype(y_ref.dtype)


def _gmlp_in(x2, g, w_bf, ln_g, ln_b, w_sp, b_sp_tile):
    t, d = x2.shape
    width = w_bf.shape[1] // 2
    tm = GMLP_TILE
    limit = _vmem_limit(
        [_nbytes((tm, d), F32), _nbytes((tm, width), BF16)],
        [_nbytes(w_bf.shape, BF16), _nbytes(w_sp.shape, F32), _nbytes(b_sp_tile.shape, F32)],
        [_nbytes((tm, d), BF16), _nbytes((tm, width), F32), _nbytes((tm, width), BF16),
         8 * _nbytes((tm, width // GMLP_GROUPS), F32)])
    return pl.pallas_call(
        _gmlp_in_kernel,
        grid=(t // tm,),
        in_specs=[pl.BlockSpec((tm, d), lambda i: (i, 0)), _resident((1, d)),
                  _resident(w_bf.shape), _resident((1, width)), _resident((1, width)),
                  _resident(w_sp.shape), _resident(b_sp_tile.shape)],
        out_specs=pl.BlockSpec((tm, width), lambda i: (i, 0)),
        out_shape=jax.ShapeDtypeStruct((t, width), BF16),
        scratch_shapes=[pltpu.VMEM((tm, d), BF16), pltpu.VMEM((tm, width), F32),
                        pltpu.VMEM((tm, width), BF16)],
        compiler_params=pltpu.CompilerParams(
            dimension_semantics=("parallel",), vmem_limit_bytes=limit),
        name="gmlp_in",
    )(x2, g, w_bf, ln_g, ln_b, w_sp, b_sp_tile)


def _mix_ffn_kernel(n_mix, *refs):
    x_ref = refs[0]
    y_refs = refs[1:1 + n_mix]
    wm_refs = refs[1 + n_mix:1 + 2 * n_mix]
    (g_ref, wg_ref, wu_ref, dww_ref, dwb_ref, wd_ref,
     o_ref, h_ref, act_ref, carry_ref) = refs[1 + 2 * n_mix:]
    tm = x_ref.shape[0]
    d_ff = wg_ref.shape[1]
    fc = FF_CHUNK

    @pl.when(pl.program_id(1) == 0)
    def _():
        carry_ref[...] = jnp.zeros_like(carry_ref)

    x1 = x_ref[...]
    for y_ref, wm_ref in zip(y_refs, wm_refs):
        x1 = x1 + _dot(y_ref[...], wm_ref[...])
    o_ref[...] = x1
    h_ref[...] = (x1 * g_ref[...]).astype(BF16)
    inv_rms = lax.rsqrt(jnp.mean(x1 * x1, axis=-1, keepdims=True) + EPS)

    for c in range(0, d_ff, fc):
        cols = slice(c, c + fc)
        gate = _dot(h_ref[...], wg_ref[:, cols]) * inv_rms
        up = _dot(h_ref[...], wu_ref[:, cols]) * inv_rms
        ext = jnp.concatenate([carry_ref[:, cols], gate], axis=0)
        carry_ref[:, cols] = gate[tm - SUBLANES:tm, :]
        conv = dwb_ref[:, cols] + dww_ref[FFN_TAPS - 1:FFN_TAPS, cols] * gate
        for k in range(FFN_TAPS - 2, -1, -1):
            ext = pltpu.roll(ext, 1, axis=0)
            conv = conv + dww_ref[k:k + 1, cols] * ext[SUBLANES:SUBLANES + tm]
        act_ref[:, cols] = (conv * jax.nn.sigmoid(conv) * up).astype(BF16)

    o_ref[...] = o_ref[...] + _dot(act_ref[...], wd_ref[...])


def _mix_ffn(x2, batch, mix_pairs, g, wg, wu, dww, dwb, wd):
    t, d = x2.shape
    d_ff = wg.shape[1]
    fc = FF_CHUNK
    n_mix = len(mix_pairs)
    ys = [y for y, _ in mix_pairs]
    wms = [w for _, w in mix_pairs]
    resident = ([_nbytes(w.shape, BF16) for w in wms] + 3 * [_nbytes(wg.shape, BF16)] +
                [_nbytes((16, d_ff), F32)])

    def blocks(rows):
        pipelined = 2 * [_nbytes((rows, d), F32)] + [_nbytes((rows, y.shape[1]), BF16) for y in ys]
        scratch = [_nbytes((rows, d), BF16), _nbytes((rows, d_ff), BF16),
                   _nbytes((SUBLANES, d_ff), F32), 4 * _nbytes((rows, fc), F32)]
        return pipelined, scratch

    tm = ROW_TILE
    for rows in (2 * ROW_TILE,):
        pipelined, scratch = blocks(rows)
        planned = 2 * sum(pipelined) + sum(resident) + sum(scratch)
        if planned <= (1.0 - FFN_VMEM_HEADROOM) * VMEM_LIMIT_CAP_BYTES and (t // batch) % rows == 0:
            tm = rows
    nt = t // batch // tm
    row = lambda width: pl.BlockSpec((tm, width), lambda b, i: (b * nt + i, 0))
    limit = _vmem_limit(*blocks(tm)[:1], resident, blocks(tm)[1])
    return pl.pallas_call(
        functools.partial(_mix_ffn_kernel, n_mix),
        grid=(batch, nt),
        in_specs=[row(d)] + [row(y.shape[1]) for y in ys] + [_resident(w.shape) for w in wms] +
                 [_resident((1, d)), _resident(wg.shape), _resident(wu.shape),
                  _resident(dww.shape), _resident(dwb.shape), _resident(wd.shape)],
        out_specs=row(d),
        out_shape=jax.ShapeDtypeStruct((t, d), F32),
        scratch_shapes=[pltpu.VMEM((tm, d), BF16), pltpu.VMEM((tm, d_ff), BF16),
                        pltpu.VMEM((SUBLANES, d_ff), F32)],
        compiler_params=pltpu.CompilerParams(
            dimension_semantics=("arbitrary", "arbitrary"), vmem_limit_bytes=limit),
        name="mix_ffn",
    )(x2, *ys, *wms, g, wg, wu, dww, dwb, wd)


def kernel(x, mix_norm_g, w_in_e, conv_w, conv_b, conv_ln_g, conv_ln_b, qn_g, kn_g, rel_bias,
           w_out_e, w_in_o, gate_ln_g, gate_ln_b, w_sp, b_sp, w_out_o, ffn_norm_g, w_gate, w_up,
           ffn_dw_w, ffn_dw_b, w_down):
    batch, seq, d = x.shape
    depth = mix_norm_g.shape[0]
    d_ff = w_gate.shape[-1]
    assert seq % ROW_TILE == 0 and seq % (ATT_SUBS * ATT_TILE) == 0 and d_ff % FF_CHUNK == 0
    assert seq % GMLP_TILE == 0 and GMLP_TILE % GMLP_CHUNK == 0
    assert seq % EVEN_TILE == 0 and seq % CONV_TILE == 0
    assert CONV_TILE % CONV_HALO == 0 and CONV_TILE % CONV_SUB == 0 and ATT_TILE % CHUNK == 0
    assert ATT_KEYS >= ATT_TILE + N_LEFT_CHUNKS * CHUNK and CONV_HALO >= CONV_TAPS - 1

    x2 = x.reshape(batch * seq, d)
    row = lambda vec: vec.reshape(1, -1).astype(F32)

    for layer in range(depth):
        i = layer // 2
        if layer % 2 == 0:
            a, q, k, v = _even_in(
                x2, row(mix_norm_g[layer]), w_in_e[i].astype(BF16),
                row(jnp.tile(qn_g[i], ATT_HEADS)), row(jnp.tile(kn_g[i], ATT_HEADS)))
            taps_w = jnp.broadcast_to(conv_w[i].astype(F32)[:, None, :],
                                      (conv_w.shape[1], SUBLANES, conv_w.shape[2]))
            a = _conv_module(a, batch, taps_w, row(conv_b[i]),
                             row(conv_ln_g[i]), row(conv_ln_b[i]))
            o = _attention(q, k, v, batch, _attn_bias_table(rel_bias[i]))
            cw = a.shape[1]
            w_out = w_out_e[i].astype(BF16)
            mix_pairs = [(a, w_out[:cw]), (o, w_out[cw:])]
        else:
            b_tile = jnp.broadcast_to(b_sp[i].astype(F32)[:, :, None],
                                      (GMLP_GROUPS, GMLP_CHUNK, LANES))
            y = _gmlp_in(x2, row(mix_norm_g[layer]), w_in_o[i].astype(BF16),
                         row(gate_ln_g[i]), row(gate_ln_b[i]), w_sp[i].astype(F32), b_tile)
            mix_pairs = [(y, w_out_o[i].astype(BF16))]
        x2 = _mix_ffn(
            x2, batch, mix_pairs, row(ffn_norm_g[layer]),
            w_gate[layer].astype(BF16), w_up[layer].astype(BF16),
            ffn_dw_w[layer].astype(F32), row(ffn_dw_b[layer]), w_down[layer].astype(BF16))
    return x2.reshape(batch, seq, d)
```
